```python
import math
import jax, jax.numpy as jnp
from jax import lax
import numpy as np

D_MODEL = 1024
BATCH = 4
SEQ = 4096
DEPTH = 2
DEC_BATCH = 128
DEC_SEQ = 8
PAST_LEN = 2048
PAGE_SIZE = 128

MIX_WIDTH = D_MODEL
N_GROUPS = 4
W_GROUP = MIX_WIDTH // N_GROUPS
H_LRU = 4
LRU_CONV = 4
LRU_C = 8.0
SC_CONV = 3
H_FOX = 4
DH_FOX = W_GROUP // H_FOX
H_DIFF = 4
DV_DIFF = W_GROUP // H_DIFF
DH_DIFF = DV_DIFF // 2
N_BUCKETS = 32
MAX_DISTANCE = 128
Q_BLOCK = 128
D_FF = -(-8 * D_MODEL // (3 * 256)) * 256
EPS = 1e-6
IN_SIZES = (W_GROUP, W_GROUP,
            W_GROUP, W_GROUP, W_GROUP,
            H_FOX * DH_FOX, H_FOX * DH_FOX, H_FOX * DH_FOX, H_FOX,
            H_DIFF * 2 * DH_DIFF, H_DIFF * 2 * DH_DIFF, H_DIFF * DV_DIFF)
N_IN = sum(IN_SIZES)

kernel_name = 'hybrid_parallel_groups_step'


def rmsnorm(x, g):
    xf = x.astype(jnp.float32)
    y = xf * lax.rsqrt(jnp.mean(xf * xf, axis=-1, keepdims=True) + EPS)
    return (y * g.astype(jnp.float32)).astype(x.dtype)


def split_columns(z):
    points = np.cumsum(np.array(IN_SIZES))[:-1].tolist()
    return jnp.split(z, points, axis=-1)


def causal_dwconv(x, buf, w):
    k_w = w.shape[0]
    t = x.shape[1]
    xp = jnp.concatenate([buf.astype(x.dtype), x], axis=1)
    y = xp[:, 0:t] * w[0]
    for k in range(1, k_w):
        y = y + xp[:, k:k + t] * w[k]
    return y, xp[:, t:]


def rg_lru(x, h0, w_a, b_a, w_x, b_x, lam):
    b, t, c = x.shape
    xb = x.reshape(b, t, H_LRU, c // H_LRU)
    r = jax.nn.sigmoid(jnp.einsum('bthi,hij->bthj', xb, w_a).reshape(b, t, c) + b_a).astype(jnp.float32)
    i = jax.nn.sigmoid(jnp.einsum('bthi,hij->bthj', xb, w_x).reshape(b, t, c) + b_x).astype(jnp.float32)
    log_a = -LRU_C * r * jax.nn.softplus(-lam.astype(jnp.float32))
    a = jnp.exp(log_a)
    u = jnp.sqrt(-jnp.expm1(2.0 * log_a)) * i * x.astype(jnp.float32)

    def step(h, au):
        a_t, u_t = au
        h = a_t * h + u_t
        return h, h

    h_last, hs = lax.scan(step, h0.astype(jnp.float32), (jnp.swapaxes(a, 0, 1), jnp.swapaxes(u, 0, 1)))
    return jnp.swapaxes(hs, 0, 1), h_last


def t5_bucket(rel):
    n = jnp.maximum(rel, 0)
    max_exact = N_BUCKETS // 2
    nf = jnp.maximum(n, 1).astype(jnp.float32)
    large = max_exact + (jnp.log(nf / max_exact) / math.log(MAX_DISTANCE / max_exact)
                         * (N_BUCKETS - max_exact)).astype(jnp.int32)
    large = jnp.minimum(large, N_BUCKETS - 1)
    return jnp.where(n < max_exact, n, large)


def sweep_query_blocks(block_fn, q_inputs, q_offset):
    t_q = q_inputs[0].shape[1]
    qb = min(Q_BLOCK, t_q)
    nb = -(-t_q // qb)
    pad = nb * qb - t_q

    def to_blocks(a):
        a = jnp.pad(a, [(0, 0), (0, pad)] + [(0, 0)] * (a.ndim - 2))
        a = a.reshape((a.shape[0], nb, qb) + a.shape[2:])
        return jnp.moveaxis(a, 1, 0)

    blocks = tuple(to_blocks(a) for a in q_inputs)
    pos = q_offset + jnp.arange(nb * qb, dtype=jnp.int32).reshape(nb, qb)
    out = lax.map(lambda args: block_fn(*args), blocks + (pos,))
    out = jnp.moveaxis(out, 0, 1)
    out = out.reshape((out.shape[0], nb * qb) + out.shape[3:])
    return out[:, :t_q]


def fox_attention(q, k, v, f_q, f_k, q_offset):
    scale = DH_FOX ** -0.5
    k_pos = jnp.arange(k.shape[1], dtype=jnp.int32)
    fk_t = jnp.swapaxes(f_k, 1, 2)[:, :, None, :]

    def block(qb, fqb, pos):
        s = jnp.einsum('bqhd,bkhd->bhqk', qb, k, preferred_element_type=jnp.float32) * scale
        s = s + jnp.swapaxes(fqb, 1, 2)[..., None] - fk_t
        s = jnp.where(k_pos[None, :] <= pos[:, None], s, -jnp.inf)
        p = jax.nn.softmax(s, axis=-1)
        return jnp.einsum('bhqk,bkhd->bqhd', p.astype(v.dtype), v)

    return sweep_query_blocks(block, (q, f_q), q_offset)


def diff_attention(q, k, v, lam, rel_bias, q_offset):
    scale = DH_DIFF ** -0.5
    k_pos = jnp.arange(k.shape[1], dtype=jnp.int32)
    k1, k2 = k[..., :DH_DIFF], k[..., DH_DIFF:]

    def block(qb, pos):
        rel = pos[:, None] - k_pos[None, :]
        bias = jnp.moveaxis(rel_bias[t5_bucket(rel)], -1, 0).astype(jnp.float32)
        mask = rel >= 0

        def probs(qq, kk):
            s = jnp.einsum('bqhd,bkhd->bhqk', qq, kk, preferred_element_type=jnp.float32) * scale + bias
            return jax.nn.softmax(jnp.where(mask, s, -jnp.inf), axis=-1)

        a = probs(qb[..., :DH_DIFF], k1) - lam * probs(qb[..., DH_DIFF:], k2)
        return jnp.einsum('bhqk,bkhd->bqhd', a, v.astype(jnp.float32))

    return sweep_query_blocks(block, (q,), q_offset)


def lambda_init_fn(layer_idx):
    return 0.8 - 0.6 * math.exp(-0.3 * layer_idx)


def mixer_block(xn, w_in, w_out, lru_conv_w, lru_conv_b, lru_w_a, lru_b_a, lru_w_x, lru_b_x, lru_lambda,
                sc_conv_w, fox_b_f, diff_lambda, diff_norm_g, rel_bias, lambda_init,
                lru_h0, lru_buf, sc_buf, fox_k_past, fox_v_past, fox_logf_past, diff_k_past, diff_v_past):
    b, t, _ = xn.shape
    past = fox_k_past.shape[1]
    z = xn @ w_in
    (x_lru, g_lru, sc_b, sc_c, sc_h, fq, fk, fv, f_logit, dq, dk, dv) = split_columns(z)

    xc, lru_buf_new = causal_dwconv(x_lru, lru_buf, lru_conv_w)
    xc = xc + lru_conv_b
    hs, h_last = rg_lru(xc, lru_h0, lru_w_a, lru_b_a, lru_w_x, lru_b_x, lru_lambda)
    y_lru = hs.astype(xn.dtype) * jax.nn.gelu(g_lru)

    cx = sc_c * sc_h
    conv, sc_buf_new = causal_dwconv(cx, sc_buf, sc_conv_w)
    y_sc = sc_b * conv

    fq = fq.reshape(b, t, H_FOX, DH_FOX)
    fk = fk.reshape(b, t, H_FOX, DH_FOX)
    fv = fv.reshape(b, t, H_FOX, DH_FOX)
    logf = jax.nn.log_sigmoid((f_logit + fox_b_f).astype(jnp.float32))
    k_all = jnp.concatenate([fox_k_past.astype(fk.dtype), fk], axis=1)
    v_all = jnp.concatenate([fox_v_past.astype(fv.dtype), fv], axis=1)
    cum_f = jnp.cumsum(jnp.concatenate([fox_logf_past.astype(jnp.float32), logf], axis=1), axis=1)
    y_fox = fox_attention(fq, k_all, v_all, cum_f[:, past:], cum_f, past).reshape(b, t, H_FOX * DH_FOX)

    dq = dq.reshape(b, t, H_DIFF, 2 * DH_DIFF)
    dk = dk.reshape(b, t, H_DIFF, 2 * DH_DIFF)
    dv = dv.reshape(b, t, H_DIFF, DV_DIFF)
    lam_par = diff_lambda.astype(jnp.float32)
    lam = (jnp.exp(jnp.sum(lam_par[0] * lam_par[1])) - jnp.exp(jnp.sum(lam_par[2] * lam_par[3]))
           + lambda_init)
    dk_all = jnp.concatenate([diff_k_past.astype(dk.dtype), dk], axis=1)
    dv_all = jnp.concatenate([diff_v_past.astype(dv.dtype), dv], axis=1)
    od = diff_attention(dq, dk_all, dv_all, lam, rel_bias, past)
    od = rmsnorm(od, diff_norm_g) * (1.0 - lambda_init)
    y_diff = od.reshape(b, t, H_DIFF * DV_DIFF).astype(xn.dtype)

    mix = jnp.concatenate([y_lru, y_sc, y_fox.astype(xn.dtype), y_diff], axis=-1) @ w_out
    return mix, (fk, fv, logf, dk, dv, h_last, lru_buf_new, sc_buf_new)


def swiglu(x, w_gate, w_up, w_down):
    return (jax.nn.silu(x @ w_gate) * (x @ w_up)) @ w_down


def gather_pages(cache_l, page_table):
    g = cache_l[page_table]
    return g.reshape((page_table.shape[0], page_table.shape[1] * cache_l.shape[1]) + cache_l.shape[2:])


def setup_inputs(seed: int = 0) -> dict:
    key = jax.random.key(seed)
    ks = iter(jax.random.split(key, 48))

    def nrm(shape, s=1.0):
        return s * jax.random.normal(next(ks), shape, jnp.float32)

    n_pages = PAST_LEN // PAGE_SIZE
    n_used = DEC_BATCH * n_pages
    n_pool = n_used + (-(-n_used // 4))
    page_table = jax.random.permutation(next(ks), n_pool)[:n_used].reshape(DEC_BATCH, n_pages).astype(jnp.int32)

    u = jax.random.uniform(next(ks), (DEPTH, W_GROUP), jnp.float32, minval=0.9, maxval=0.999)
    s_a = u ** (1.0 / LRU_C)
    lru_lambda = jnp.log(s_a) - jnp.log1p(-s_a)
    gb = W_GROUP // H_LRU
    return {
        'x_prompt': nrm((BATCH, SEQ, D_MODEL)),
        'x_sample': nrm((DEC_BATCH, DEC_SEQ, D_MODEL)),
        'cache_fox_k': nrm((DEPTH, n_pool, PAGE_SIZE, H_FOX, DH_FOX)),
        'cache_fox_v': nrm((DEPTH, n_pool, PAGE_SIZE, H_FOX, DH_FOX)),
        'cache_fox_logf': jax.nn.log_sigmoid(2.0 + nrm((DEPTH, n_pool, PAGE_SIZE, H_FOX))),
        'cache_diff_k': nrm((DEPTH, n_pool, PAGE_SIZE, H_DIFF, 2 * DH_DIFF)),
        'cache_diff_v': nrm((DEPTH, n_pool, PAGE_SIZE, H_DIFF, DV_DIFF)),
        'state_lru_h': nrm((DEPTH, DEC_BATCH, W_GROUP), 0.3),
        'state_lru_conv': nrm((DEPTH, DEC_BATCH, LRU_CONV - 1, W_GROUP)),
        'state_sconv': nrm((DEPTH, DEC_BATCH, SC_CONV - 1, W_GROUP)),
        'page_table': page_table,
        'norm_mix_g': 1.0 + nrm((DEPTH, D_MODEL), 0.01),
        'w_in': nrm((DEPTH, D_MODEL, N_IN), D_MODEL ** -0.5),
        'w_out': nrm((DEPTH, MIX_WIDTH, D_MODEL), MIX_WIDTH ** -0.5),
        'lru_conv_w': nrm((DEPTH, LRU_CONV, W_GROUP), LRU_CONV ** -0.5),
        'lru_conv_b': nrm((DEPTH, W_GROUP), 0.01),
        'lru_w_a': nrm((DEPTH, H_LRU, gb, gb), gb ** -0.5),
        'lru_b_a': nrm((DEPTH, W_GROUP), 0.01),
        'lru_w_x': nrm((DEPTH, H_LRU, gb, gb), gb ** -0.5),
        'lru_b_x': nrm((DEPTH, W_GROUP), 0.01),
        'lru_lambda': lru_lambda,
        'sc_conv_w': nrm((DEPTH, SC_CONV, W_GROUP), SC_CONV ** -0.5),
        'fox_b_f': 2.0 + nrm((DEPTH, H_FOX), 0.1),
        'diff_lambda': nrm((DEPTH, 4, DH_DIFF), 0.1),
        'diff_norm_g': 1.0 + nrm((DEPTH, DV_DIFF), 0.01),
        'rel_bias': nrm((N_BUCKETS, H_DIFF), 0.5),
        'norm_ffn_g': 1.0 + nrm((DEPTH, D_MODEL), 0.01),
        'w_gate': nrm((DEPTH, D_MODEL, D_FF), D_MODEL ** -0.5),
        'w_up': nrm((DEPTH, D_MODEL, D_FF), D_MODEL ** -0.5),
        'w_down': nrm((DEPTH, D_FF, D_MODEL), D_FF ** -0.5),
        'norm_final_g': 1.0 + nrm((D_MODEL,), 0.01),
    }


def reference(x_prompt, x_sample, cache_fox_k, cache_fox_v, cache_fox_logf, cache_diff_k, cache_diff_v,
              state_lru_h, state_lru_conv, state_sconv, page_table,
              norm_mix_g, w_in, w_out, lru_conv_w, lru_conv_b, lru_w_a, lru_b_a, lru_w_x, lru_b_x,
              lru_lambda, sc_conv_w, fox_b_f, diff_lambda, diff_norm_g, rel_bias,
              norm_ffn_g, w_gate, w_up, w_down, norm_final_g):

    def layer(x, l, lru_h0, lru_buf, sc_buf, fk_p, fv_p, flf_p, dk_p, dv_p):
        xn = rmsnorm(x, norm_mix_g[l])
        mix, st = mixer_block(xn, w_in[l], w_out[l], lru_conv_w[l], lru_conv_b[l], lru_w_a[l], lru_b_a[l],
                              lru_w_x[l], lru_b_x[l], lru_lambda[l], sc_conv_w[l], fox_b_f[l],
                              diff_lambda[l], diff_norm_g[l], rel_bias, lambda_init_fn(l),
                              lru_h0, lru_buf, sc_buf, fk_p, fv_p, flf_p, dk_p, dv_p)
        x = x + mix
        x = x + swiglu(rmsnorm(x, norm_ffn_g[l]), w_gate[l], w_up[l], w_down[l])
        return x, st

    b = x_prompt.shape[0]
    dt = x_prompt.dtype
    hp = x_prompt
    p_states = []
    for l in range(DEPTH):
        hp, st = layer(hp, l,
                       jnp.zeros((b, W_GROUP), dt),
                       jnp.zeros((b, LRU_CONV - 1, W_GROUP), dt),
                       jnp.zeros((b, SC_CONV - 1, W_GROUP), dt),
                       jnp.zeros((b, 0, H_FOX, DH_FOX), dt),
                       jnp.zeros((b, 0, H_FOX, DH_FOX), dt),
                       jnp.zeros((b, 0, H_FOX), jnp.float32),
                       jnp.zeros((b, 0, H_DIFF, 2 * DH_DIFF), dt),
                       jnp.zeros((b, 0, H_DIFF, DV_DIFF), dt))
        p_states.append(st)
    y_prompt = rmsnorm(hp, norm_final_g)

    hs_ = x_sample
    s_states = []
    for l in range(DEPTH):
        hs_, st = layer(hs_, l, state_lru_h[l], state_lru_conv[l], state_sconv[l],
                        gather_pages(cache_fox_k[l], page_table),
                        gather_pages(cache_fox_v[l], page_table),
                        gather_pages(cache_fox_logf[l], page_table),
                        gather_pages(cache_diff_k[l], page_table),
                        gather_pages(cache_diff_v[l], page_table))
        s_states.append(st)
    y_sample = rmsnorm(hs_, norm_final_g)

    (p_fox_k, p_fox_v, p_fox_logf, p_diff_k, p_diff_v, p_lru_h, p_lru_conv, p_sconv) = [
        jnp.stack(s, axis=0) for s in zip(*p_states)]
    (s_fox_k, s_fox_v, s_fox_logf, s_diff_k, s_diff_v, s_lru_h, s_lru_conv, s_sconv) = [
        jnp.stack(s, axis=0) for s in zip(*s_states)]
    return (y_prompt, y_sample,
            p_fox_k, p_fox_v, p_fox_logf, p_diff_k, p_diff_v, p_lru_h, p_lru_conv, p_sconv,
            s_fox_k, s_fox_v, s_fox_logf, s_diff_k, s_diff_v, s_lru_h, s_lru_conv, s_sconv)
```

```python
import functools
import math

import jax
import jax.numpy as jnp
from jax import lax
from jax.experimental import pallas as pl
from jax.experimental.pallas import tpu as pltpu

F32 = jnp.float32
BF16 = jnp.bfloat16

N_HEADS = 4
W_GROUP = 256
DH = W_GROUP // N_HEADS
DH_DIFF = DH // 2
LRU_CONV = 4
SC_CONV = 3
LRU_C = 8.0
N_BUCKETS = 32
MAX_DISTANCE = 128
EPS = 1e-6
FOX_SCALE = DH ** -0.5
DIFF_SCALE = DH_DIFF ** -0.5
NEG = -1e30
DH_SHIFT = DH.bit_length() - 1
DHD_SHIFT = DH_DIFF.bit_length() - 1
assert DH == 1 << DH_SHIFT and DH_DIFF == 1 << DHD_SHIFT

LANES = 128
SUBLANES = 8
VMEM_LIMIT = 48 * 1024 * 1024

C_LRU = (0, 2 * W_GROUP)
C_SC = (C_LRU[1], C_LRU[1] + 3 * W_GROUP)
C_FQ = (C_SC[1], C_SC[1] + W_GROUP)
C_FK = (C_FQ[1], C_FQ[1] + W_GROUP)
C_FV = (C_FK[1], C_FK[1] + W_GROUP)
C_DQ = (C_FV[1], C_FV[1] + W_GROUP)
C_DK = (C_DQ[1], C_DQ[1] + W_GROUP)
C_DV = (C_DK[1], C_DK[1] + W_GROUP)
C_FL = (C_DV[1], C_DV[1] + LANES)
N_IN_PAD = C_FL[1]


def _lambda_init(layer_idx):
    return 0.8 - 0.6 * math.exp(-0.3 * layer_idx)


def _params(sem):
    return pltpu.CompilerParams(dimension_semantics=sem, vmem_limit_bytes=VMEM_LIMIT)


def _inproj_body(x_ref, g_ref, w_ref, zl_ref, zs_ref, fl_ref, fqb_ref, fkb_ref, fvb_ref,
                 dqb_ref, dkb_ref, dvb_ref, fk_ref, fv_ref, dk_ref, dv_ref, xn_ref):
    x = x_ref[...]
    xn = x * lax.rsqrt(jnp.mean(x * x, axis=-1, keepdims=True) + EPS) * g_ref[...]
    xn_ref[...] = xn.astype(BF16)

    def seg(c):
        return jnp.dot(xn_ref[...], w_ref[:, c[0]:c[1]], preferred_element_type=F32)

    zl_ref[...] = seg(C_LRU)
    zs_ref[...] = seg(C_SC)
    fl_ref[...] = seg(C_FL)
    fqb_ref[...] = (seg(C_FQ) * FOX_SCALE).astype(BF16)
    dqb_ref[...] = (seg(C_DQ) * DIFF_SCALE).astype(BF16)
    for c, o32, ob in ((C_FK, fk_ref, fkb_ref), (C_FV, fv_ref, fvb_ref),
                       (C_DK, dk_ref, dkb_ref), (C_DV, dv_ref, dvb_ref)):
        z = seg(c)
        o32[...] = z
        ob[...] = z.astype(BF16)


def _inproj(x, g, w, tm):
    n, d = x.shape
    row = lambda i: (i, 0)
    const = lambda i: (0, 0)
    widths = [(2 * W_GROUP, F32), (3 * W_GROUP, F32), (LANES, F32)] + [(W_GROUP, BF16)] * 6 + [(W_GROUP, F32)] * 4
    return pl.pallas_call(
        _inproj_body,
        grid=(n // tm,),
        in_specs=[pl.BlockSpec((tm, d), row), pl.BlockSpec((1, d), const), pl.BlockSpec((d, N_IN_PAD), const)],
        out_specs=[pl.BlockSpec((tm, wd), row) for wd, _ in widths],
        out_shape=[jax.ShapeDtypeStruct((n, wd), dt) for wd, dt in widths],
        scratch_shapes=[pltpu.VMEM((tm, d), BF16)],
        compiler_params=_params(("parallel",)),
        name="inproj",
    )(x, g, w)


def _shift_rows(x, s, fill):
    row = lax.broadcasted_iota(jnp.int32, x.shape, 0)
    return jnp.where(row >= s, pltpu.roll(x, s, axis=0), fill)


def _mix_body(zl_ref, zs_ref, fl_ref, h0_ref, lbuf_ref, sbuf_ref, cw_ref, cb_ref, wg_ref, bg_ref, lam_ref,
              scw_ref, bf_ref,
              ylru_ref, ysc_ref, logf_ref, cumf_ref, hlast_ref, lbufn_ref, sbufn_ref,
              xpl_ref, xps_ref, hc_ref, fc_ref, *, tc):
    i = pl.program_id(1)
    last = pl.num_programs(1) - 1
    pad = SUBLANES

    @pl.when(i == 0)
    def _():
        xpl_ref[0:pad, :] = jnp.zeros((pad, W_GROUP), F32)
        xps_ref[0:pad, :] = jnp.zeros((pad, W_GROUP), F32)
        xpl_ref[pad - (LRU_CONV - 1):pad, :] = lbuf_ref[0]
        xps_ref[pad - (SC_CONV - 1):pad, :] = sbuf_ref[0]
        hc_ref[...] = h0_ref[0]
        fc_ref[...] = jnp.zeros_like(fc_ref)

    xpl_ref[pad:pad + tc, :] = zl_ref[:, 0:W_GROUP]
    o = pad - (LRU_CONV - 1)
    xc = xpl_ref[o:o + tc, :] * cw_ref[0:1, :]
    for k in range(1, LRU_CONV):
        xc = xc + xpl_ref[o + k:o + k + tc, :] * cw_ref[k:k + 1, :]
    xc = xc + cb_ref[...]
    gz = jnp.dot(xc.astype(BF16), wg_ref[...], preferred_element_type=F32) + bg_ref[...]
    r = jax.nn.sigmoid(gz[:, 0:W_GROUP])
    ig = jax.nn.sigmoid(gz[:, W_GROUP:2 * W_GROUP])
    nl = -lam_ref[...]
    sp = jnp.maximum(nl, 0.0) + jnp.log1p(jnp.exp(-jnp.abs(nl)))
    log_a = -LRU_C * r * sp
    a = jnp.exp(log_a)
    u = jnp.sqrt(1.0 - a * a) * ig * xc
    s = 1
    while s < tc:
        u = a * _shift_rows(u, s, 0.0) + u
        a = a * _shift_rows(a, s, 1.0)
        s *= 2
    h = a * hc_ref[...] + u
    hc_ref[...] = h[tc - 1:tc, :]
    ylru_ref[...] = (h * jax.nn.gelu(zl_ref[:, W_GROUP:2 * W_GROUP])).astype(BF16)

    xps_ref[pad:pad + tc, :] = zs_ref[:, W_GROUP:2 * W_GROUP] * zs_ref[:, 2 * W_GROUP:3 * W_GROUP]
    o = pad - (SC_CONV - 1)
    conv = xps_ref[o:o + tc, :] * scw_ref[0:1, :]
    for k in range(1, SC_CONV):
        conv = conv + xps_ref[o + k:o + k + tc, :] * scw_ref[k:k + 1, :]
    ysc_ref[...] = (zs_ref[:, 0:W_GROUP] * conv).astype(BF16)

    xf = fl_ref[...] + bf_ref[...]
    logf = jnp.minimum(xf, 0.0) - jnp.log1p(jnp.exp(-jnp.abs(xf)))
    logf_ref[...] = logf
    c = logf
    s = 1
    while s < tc:
        c = c + _shift_rows(c, s, 0.0)
        s *= 2
    c = c + fc_ref[...]
    cumf_ref[...] = c
    fc_ref[...] = c[tc - 1:tc, :]

    @pl.when(i == last)
    def _():
        hlast_ref[0] = hc_ref[...]
        lbufn_ref[0] = xpl_ref[pad + tc - (LRU_CONV - 1):pad + tc, :]
        sbufn_ref[0] = xps_ref[pad + tc - (SC_CONV - 1):pad + tc, :]

    xpl_ref[0:pad, :] = xpl_ref[tc:tc + pad, :]
    xps_ref[0:pad, :] = xps_ref[tc:tc + pad, :]


def _mix(zl, zs, fl, h0, lbuf, sbuf, cw, cb, wg, bg, lam, scw, bf, g, t, tc):
    n = g * t
    nt = t // tc
    row = lambda b, i: (b * nt + i, 0)
    st = lambda b, i: (b, 0, 0)
    const = lambda b, i: (0, 0)
    full = lambda a: pl.BlockSpec(a.shape, const)
    return pl.pallas_call(
        functools.partial(_mix_body, tc=tc),
        grid=(g, nt),
        in_specs=[pl.BlockSpec((tc, 2 * W_GROUP), row), pl.BlockSpec((tc, 3 * W_GROUP), row),
                  pl.BlockSpec((tc, LANES), row),
                  pl.BlockSpec((1, 1, W_GROUP), st), pl.BlockSpec((1, LRU_CONV - 1, W_GROUP), st),
                  pl.BlockSpec((1, SC_CONV - 1, W_GROUP), st),
                  full(cw), full(cb), full(wg), full(bg), full(lam), full(scw), full(bf)],
        out_specs=[pl.BlockSpec((tc, W_GROUP), row), pl.BlockSpec((tc, W_GROUP), row),
                   pl.BlockSpec((tc, LANES), row), pl.BlockSpec((tc, LANES), row),
                   pl.BlockSpec((1, 1, W_GROUP), st), pl.BlockSpec((1, LRU_CONV - 1, W_GROUP), st),
                   pl.BlockSpec((1, SC_CONV - 1, W_GROUP), st)],
        out_shape=[jax.ShapeDtypeStruct((n, W_GROUP), BF16), jax.ShapeDtypeStruct((n, W_GROUP), BF16),
                   jax.ShapeDtypeStruct((n, LANES), F32), jax.ShapeDtypeStruct((n, LANES), F32),
                   jax.ShapeDtypeStruct((g, 1, W_GROUP), F32),
                   jax.ShapeDtypeStruct((g, LRU_CONV - 1, W_GROUP), F32),
                   jax.ShapeDtypeStruct((g, SC_CONV - 1, W_GROUP), F32)],
        scratch_shapes=[pltpu.VMEM((tc + SUBLANES, W_GROUP), F32), pltpu.VMEM((tc + SUBLANES, W_GROUP), F32),
                        pltpu.VMEM((1, W_GROUP), F32), pltpu.VMEM((1, LANES), F32)],
        compiler_params=_params(("parallel", "arbitrary")),
        name="mix_scan",
    )(zl, zs, fl, h0, lbuf, sbuf, cw, cb, wg, bg, lam, scw, bf)


def _bias_body(rb_ref, o_ref, *, off):
    _, rows, cols = o_ref.shape
    rel = off + lax.broadcasted_iota(jnp.int32, (rows, cols), 0) - lax.broadcasted_iota(jnp.int32, (rows, cols), 1)
    nn = jnp.maximum(rel, 0)
    max_exact = N_BUCKETS // 2
    nf = jnp.maximum(nn, 1).astype(F32)
    large = max_exact + (jnp.log(nf / max_exact) / math.log(MAX_DISTANCE / max_exact)
                         * (N_BUCKETS - max_exact)).astype(jnp.int32)
    large = jnp.minimum(large, N_BUCKETS - 1)
    bucket = jnp.where(nn < max_exact, nn, large)
    for h in range(N_HEADS):
        val = jnp.zeros((rows, cols), F32)
        for b in range(N_BUCKETS):
            val = jnp.where(bucket == b, rb_ref[b, h], val)
        o_ref[h] = jnp.where(rel >= 0, val - rb_ref[N_BUCKETS - 1, h], NEG)


def _bias_tile(rel_bias, rows, cols, off):
    return pl.pallas_call(
        functools.partial(_bias_body, off=off),
        in_specs=[pl.BlockSpec(memory_space=pltpu.SMEM)],
        out_specs=pl.BlockSpec(memory_space=pltpu.VMEM),
        out_shape=jax.ShapeDtypeStruct((N_HEADS, rows, cols), F32),
        name="t5_bias_tile",
    )(rel_bias)


def _flash_update(q, k, v, bias, m_ref, l_ref, acc_ref):
    s = lax.dot_general(q, k, (((1,), (1,)), ((), ())), preferred_element_type=F32)
    if bias is not None:
        s = s + bias
    m_prev = m_ref[...]
    m_new = jnp.maximum(m_prev, jnp.max(s, axis=-1, keepdims=True))
    alpha = jnp.exp(m_prev - m_new)
    p = jnp.exp(s - m_new)
    l_ref[...] = alpha * l_ref[...] + jnp.sum(p, axis=-1, keepdims=True)
    acc_ref[...] = alpha * acc_ref[...] + jnp.dot(p.astype(BF16), v, preferred_element_type=F32)
    m_ref[...] = m_new


def _pair_lane():
    return lax.broadcasted_iota(jnp.int32, (1, LANES), 1)


def _diff_lambda(dl_ref, lambda_init):
    dl = dl_ref[...]
    s01 = jnp.sum(dl[0:1, :] * dl[1:2, :], axis=-1, keepdims=True)
    s23 = jnp.sum(dl[2:3, :] * dl[3:4, :], axis=-1, keepdims=True)
    return jnp.exp(s01) - jnp.exp(s23) + lambda_init


def _fox_body(q_ref, k_ref, v_ref, fk_ref, o_ref, m_ref, l_ref, acc_ref, *, tq, tk):
    qi = pl.program_id(1)
    lane = _pair_lane()
    n_diag = tq // tk
    n_full = qi * n_diag
    qs = []
    for h in range(N_HEADS):
        p, e = divmod(h, 2)
        qp = q_ref[:, p * LANES:(p + 1) * LANES]
        qs.append(jnp.where((lane >> DH_SHIFT) == e, qp, jnp.zeros_like(qp)))
        m_ref[h] = jnp.full((tq, 1), NEG, F32)
        l_ref[h] = jnp.zeros((tq, 1), F32)
        acc_ref[h] = jnp.zeros((tq, LANES), F32)

    def block(k0, mask):
        for h in range(N_HEADS):
            p = h // 2
            k = k_ref[pl.ds(k0, tk), p * LANES:(p + 1) * LANES]
            v = v_ref[pl.ds(k0, tk), p * LANES:(p + 1) * LANES]
            bias = -fk_ref[h:h + 1, pl.ds(k0, tk)]
            if mask is not None:
                bias = jnp.where(mask, bias, NEG)
            _flash_update(qs[h], k, v, bias, m_ref.at[h], l_ref.at[h], acc_ref.at[h])

    def far(kj, carry):
        block(pl.multiple_of(kj * tk, tk), None)
        return carry

    lax.fori_loop(0, n_full, far, 0)
    row = lax.broadcasted_iota(jnp.int32, (tq, tk), 0)
    col = lax.broadcasted_iota(jnp.int32, (tq, tk), 1)
    for d in range(n_diag):
        block(pl.multiple_of(qi * tq + d * tk, tk), col + d * tk <= row)

    for p in range(N_HEADS // 2):
        o0 = acc_ref[2 * p] / l_ref[2 * p]
        o1 = acc_ref[2 * p + 1] / l_ref[2 * p + 1]
        o_ref[:, p * LANES:(p + 1) * LANES] = jnp.where(lane < DH, o0, o1).astype(BF16)


def _fox_prompt(q, k, v, fk_rows, g, t, tq, tk):
    nq = t // tq
    return pl.pallas_call(
        functools.partial(_fox_body, tq=tq, tk=tk),
        grid=(g, nq),
        in_specs=[pl.BlockSpec((tq, W_GROUP), lambda b, i: (b * nq + i, 0)),
                  pl.BlockSpec((t, W_GROUP), lambda b, i: (b, 0)),
                  pl.BlockSpec((t, W_GROUP), lambda b, i: (b, 0)),
                  pl.BlockSpec((None, N_HEADS, t), lambda b, i: (b, 0, 0))],
        out_specs=pl.BlockSpec((tq, W_GROUP), lambda b, i: (b * nq + i, 0)),
        out_shape=jax.ShapeDtypeStruct((g * t, W_GROUP), BF16),
        scratch_shapes=[pltpu.VMEM((N_HEADS, tq, 1), F32), pltpu.VMEM((N_HEADS, tq, 1), F32),
                        pltpu.VMEM((N_HEADS, tq, LANES), F32)],
        compiler_params=_params(("parallel", "arbitrary")),
        name="fox_prompt",
    )(q, k, v, fk_rows)


def _head_rmsnorm_pair(od, lane, g_row, lambda_init):
    sq = od * od
    lo = lane < DH
    ms0 = jnp.sum(jnp.where(lo, sq, 0.0), axis=-1, keepdims=True) * (1.0 / DH)
    ms1 = jnp.sum(jnp.where(lo, 0.0, sq), axis=-1, keepdims=True) * (1.0 / DH)
    inv = jnp.where(lo, lax.rsqrt(ms0 + EPS), lax.rsqrt(ms1 + EPS))
    return od * inv * g_row * (1.0 - lambda_init)


def _diffattn_body(q_ref, k_ref, v_ref, strip_ref, dl_ref, g_ref, o_ref, m_ref, l_ref, acc_ref,
                   *, tq, tk, lambda_init):
    qi = pl.program_id(1)
    lane = _pair_lane()
    n_diag = tq // tk
    n_full = qi * n_diag
    qs = []
    for h in range(N_HEADS):
        p, e = divmod(h, 2)
        qp = q_ref[:, p * LANES:(p + 1) * LANES]
        zero = jnp.zeros_like(qp)
        q1 = jnp.where((lane >> DHD_SHIFT) == 2 * e, qp, zero)
        q2 = jnp.where((lane >> DHD_SHIFT) == 2 * e + 1, qp, zero)
        qs.append(jnp.concatenate([q1, q2], axis=0))
        m_ref[h] = jnp.full((2 * tq, 1), NEG, F32)
        l_ref[h] = jnp.zeros((2 * tq, 1), F32)
        acc_ref[h] = jnp.zeros((2 * tq, LANES), F32)

    def block(k0, c0):
        for h in range(N_HEADS):
            p = h // 2
            k = k_ref[pl.ds(k0, tk), p * LANES:(p + 1) * LANES]
            v = v_ref[pl.ds(k0, tk), p * LANES:(p + 1) * LANES]
            bias = None
            if c0 is not None:
                b1 = strip_ref[h, :, c0:c0 + tk]
                bias = jnp.concatenate([b1, b1], axis=0)
            _flash_update(qs[h], k, v, bias, m_ref.at[h], l_ref.at[h], acc_ref.at[h])

    def far(kj, carry):
        block(pl.multiple_of(kj * tk, tk), None)
        return carry

    lax.fori_loop(0, jnp.maximum(n_full - 1, 0), far, 0)

    @pl.when(n_full > 0)
    def _():
        block(pl.multiple_of((n_full - 1) * tk, tk), 0)

    for d in range(n_diag):
        block(pl.multiple_of(qi * tq + d * tk, tk), (d + 1) * tk)

    lam = _diff_lambda(dl_ref, lambda_init)
    for p in range(N_HEADS // 2):
        ods = []
        for h in (2 * p, 2 * p + 1):
            o = acc_ref[h] / l_ref[h]
            ods.append(o[0:tq] - lam * o[tq:2 * tq])
        od = jnp.where(lane < DH, ods[0], ods[1])
        y = _head_rmsnorm_pair(od, lane, g_ref[...], lambda_init)
        o_ref[:, p * LANES:(p + 1) * LANES] = y.astype(BF16)


def _diff_prompt(q, k, v, strip, dlam, g_pair, g, t, tq, tk, lambda_init):
    nq = t // tq
    const2 = lambda b, i: (0, 0)
    return pl.pallas_call(
        functools.partial(_diffattn_body, tq=tq, tk=tk, lambda_init=lambda_init),
        grid=(g, nq),
        in_specs=[pl.BlockSpec((tq, W_GROUP), lambda b, i: (b * nq + i, 0)),
                  pl.BlockSpec((t, W_GROUP), lambda b, i: (b, 0)),
                  pl.BlockSpec((t, W_GROUP), lambda b, i: (b, 0)),
                  pl.BlockSpec(strip.shape, lambda b, i: (0, 0, 0)),
                  pl.BlockSpec(dlam.shape, const2), pl.BlockSpec(g_pair.shape, const2)],
        out_specs=pl.BlockSpec((tq, W_GROUP), lambda b, i: (b * nq + i, 0)),
        out_shape=jax.ShapeDtypeStruct((g * t, W_GROUP), BF16),
        scratch_shapes=[pltpu.VMEM((N_HEADS, 2 * tq, 1), F32), pltpu.VMEM((N_HEADS, 2 * tq, 1), F32),
                        pltpu.VMEM((N_HEADS, 2 * tq, LANES), F32)],
        compiler_params=_params(("parallel", "arbitrary")),
        name="diff_prompt",
    )(q, k, v, strip, dlam, g_pair)


def _sample_attn_body(pt_ref, fq_ref, dq_ref, fkn_ref, fvn_ref, dkn_ref, dvn_ref, ncum_ref, t15_ref, tnew_ref,
                      dl_ref, g_ref, *rest, ts, pp, page, lambda_init):
    del pt_ref
    pages = rest[:5 * pp]
    yf_ref, yd_ref = rest[5 * pp:5 * pp + 2]
    mf_ref, lf_ref, af_ref, md_ref, ld_ref, ad_ref, suf_ref = rest[5 * pp + 2:]
    j = pl.program_id(1)
    last = pl.num_programs(1) - 1
    rf = N_HEADS * ts
    rd = 2 * rf

    ts_shift = ts.bit_length() - 1
    lane = lax.broadcasted_iota(jnp.int32, (rf, W_GROUP), 1)
    rowh = lax.broadcasted_iota(jnp.int32, (rf, W_GROUP), 0) >> ts_shift
    qf = fq_ref[0].astype(F32)
    qf = jnp.concatenate([qf] * N_HEADS, axis=0)
    qf = jnp.where((lane >> DH_SHIFT) == rowh, qf, 0.0).astype(BF16)
    qd = dq_ref[0].astype(F32)
    qd = jnp.concatenate([qd] * N_HEADS, axis=0)
    qd = jnp.concatenate([jnp.where((lane >> DHD_SHIFT) == 2 * rowh, qd, 0.0),
                          jnp.where((lane >> DHD_SHIFT) == 2 * rowh + 1, qd, 0.0)], axis=0).astype(BF16)

    def per_head_rows(x):
        return jnp.concatenate([jnp.broadcast_to(x[h:h + 1, :], (ts, x.shape[1])) for h in range(N_HEADS)], axis=0)

    def pad_rows(x):
        return jnp.concatenate([x, jnp.zeros((page - ts, x.shape[1]), x.dtype)], axis=0)

    @pl.when(j == 0)
    def _():
        mf_ref[...] = jnp.full_like(mf_ref, NEG)
        lf_ref[...] = jnp.zeros_like(lf_ref)
        af_ref[...] = jnp.zeros_like(af_ref)
        md_ref[...] = jnp.full_like(md_ref, NEG)
        ld_ref[...] = jnp.zeros_like(ld_ref)
        ad_ref[...] = jnp.zeros_like(ad_ref)
        suf_ref[...] = jnp.zeros_like(suf_ref)
        r = lax.broadcasted_iota(jnp.int32, (rf, page), 0) & (ts - 1)
        c = lax.broadcasted_iota(jnp.int32, (rf, page), 1)
        bias_f = jnp.where(c <= r, per_head_rows(ncum_ref[0]), NEG)
        _flash_update(qf, pad_rows(fkn_ref[0]).astype(BF16), pad_rows(fvn_ref[0]).astype(BF16), bias_f,
                      mf_ref, lf_ref, af_ref)
        tn = tnew_ref[...].reshape(rf, page)
        _flash_update(qd, pad_rows(dkn_ref[0]).astype(BF16), pad_rows(dvn_ref[0]).astype(BF16),
                      jnp.concatenate([tn, tn], axis=0), md_ref, ld_ref, ad_ref)

    klane = lax.broadcasted_iota(jnp.int32, (N_HEADS, page), 1)
    for p in range(pp):
        fk_p, fv_p, dk_p, dv_p, lf_p = pages[5 * p:5 * p + 5]
        logf = lf_p[0, 0]
        sfx = logf
        s = 1
        while s < page:
            sfx = sfx + jnp.where(klane < page - s, pltpu.roll(sfx, page - s, axis=1), 0.0)
            s *= 2
        carry = suf_ref[...]
        key_bias = sfx - logf + carry
        suf_ref[...] = carry + sfx[:, 0:1]
        _flash_update(qf, fk_p[0, 0].astype(BF16), fv_p[0, 0].astype(BF16), per_head_rows(key_bias),
                      mf_ref, lf_ref, af_ref)
        bias_d = None
        if p == 0:
            t15 = t15_ref[...].reshape(rf, page)
            bias_d = jnp.where(j == 0, jnp.concatenate([t15, t15], axis=0), 0.0)
        _flash_update(qd, dk_p[0, 0].astype(BF16), dv_p[0, 0].astype(BF16), bias_d, md_ref, ld_ref, ad_ref)

    @pl.when(j == last)
    def _():
        lane_o = lax.broadcasted_iota(jnp.int32, (ts, W_GROUP), 1)
        of = af_ref[...] / lf_ref[...]
        od_all = ad_ref[...] / ld_ref[...]
        lam = _diff_lambda(dl_ref, lambda_init)
        od_all = od_all[0:rf] - lam * od_all[rf:rd]
        yf = jnp.zeros((ts, W_GROUP), F32)
        od = jnp.zeros((ts, W_GROUP), F32)
        inv = jnp.zeros((ts, W_GROUP), F32)
        for h in range(N_HEADS):
            sel = (lane_o >> DH_SHIFT) == h
            yf = jnp.where(sel, of[h * ts:(h + 1) * ts], yf)
            oh = jnp.where(sel, od_all[h * ts:(h + 1) * ts], 0.0)
            ms = jnp.sum(oh * oh, axis=-1, keepdims=True) * (1.0 / DH)
            od = od + oh
            inv = jnp.where(sel, lax.rsqrt(ms + EPS), inv)
        yf_ref[0] = yf.astype(BF16)
        yd_ref[0] = (od * inv * g_ref[...] * (1.0 - lambda_init)).astype(BF16)


def _sample_attn(page_table, fq, dq, fkn, fvn, dkn, dvn, ncum, t15, tnew, dlam, g_full,
                 cfk, cfv, cdk, cdv, clf_t, layer, g, ts, pp, lambda_init):
    n_pages = page_table.shape[1]
    page = cfk.shape[2]
    n_steps = n_pages // pp
    rf = N_HEADS * ts

    tok = lambda b, j, pt: (b, 0, 0)
    const3 = lambda b, j, pt: (0, 0, 0)
    const2 = lambda b, j, pt: (0, 0)

    def page_map(p):
        return lambda b, j, pt: (layer, pt[b, n_pages - 1 - (j * pp + p)], 0, 0)

    in_specs = [pl.BlockSpec((1, ts, W_GROUP), tok)] * 6
    in_specs += [pl.BlockSpec((1, N_HEADS, page), tok), pl.BlockSpec(t15.shape, const3),
                 pl.BlockSpec(tnew.shape, const3), pl.BlockSpec(dlam.shape, const2),
                 pl.BlockSpec(g_full.shape, const2)]
    args = [fq, dq, fkn, fvn, dkn, dvn, ncum, t15, tnew, dlam, g_full]
    for p in range(pp):
        in_specs += [pl.BlockSpec((1, 1, page, W_GROUP), page_map(p))] * 4
        in_specs += [pl.BlockSpec((1, 1, N_HEADS, page), page_map(p))]
        args += [cfk, cfv, cdk, cdv, clf_t]
    grid_spec = pltpu.PrefetchScalarGridSpec(
        num_scalar_prefetch=1,
        grid=(g, n_steps),
        in_specs=in_specs,
        out_specs=[pl.BlockSpec((1, ts, W_GROUP), tok), pl.BlockSpec((1, ts, W_GROUP), tok)],
        scratch_shapes=[pltpu.VMEM((rf, 1), F32), pltpu.VMEM((rf, 1), F32), pltpu.VMEM((rf, W_GROUP), F32),
                        pltpu.VMEM((2 * rf, 1), F32), pltpu.VMEM((2 * rf, 1), F32),
                        pltpu.VMEM((2 * rf, W_GROUP), F32), pltpu.VMEM((N_HEADS, 1), F32)],
    )
    return pl.pallas_call(
        functools.partial(_sample_attn_body, ts=ts, pp=pp, page=page, lambda_init=lambda_init),
        grid_spec=grid_spec,
        out_shape=[jax.ShapeDtypeStruct((g, ts, W_GROUP), BF16)] * 2,
        compiler_params=_params(("parallel", "arbitrary")),
        name="sample_attn",
    )(page_table, *args)


def _ffn_body(x_ref, y0_ref, y1_ref, y2_ref, y3_ref, wo_ref, gn_ref, wg_ref, wu_ref, wd_ref, gf_ref,
              o_ref, acc_ref, xn_ref, *, final_norm):
    j = pl.program_id(1)

    @pl.when(j == 0)
    def _():
        x1 = x_ref[...]
        for gi, y_ref in enumerate((y0_ref, y1_ref, y2_ref, y3_ref)):
            x1 = x1 + jnp.dot(y_ref[...], wo_ref[gi * W_GROUP:(gi + 1) * W_GROUP, :], preferred_element_type=F32)
        acc_ref[...] = x1
        xn = x1 * lax.rsqrt(jnp.mean(x1 * x1, axis=-1, keepdims=True) + EPS) * gn_ref[...]
        xn_ref[...] = xn.astype(BF16)

    xn = xn_ref[...]
    gate = jnp.dot(xn, wg_ref[...], preferred_element_type=F32)
    up = jnp.dot(xn, wu_ref[...], preferred_element_type=F32)
    hid = (jax.nn.silu(gate) * up).astype(BF16)
    acc_ref[...] += jnp.dot(hid, wd_ref[...], preferred_element_type=F32)

    @pl.when(j == pl.num_programs(1) - 1)
    def _():
        x2 = acc_ref[...]
        if final_norm:
            x2 = x2 * lax.rsqrt(jnp.mean(x2 * x2, axis=-1, keepdims=True) + EPS) * gf_ref[...]
        o_ref[...] = x2


def _outproj_ffn(x, ys, wo, gn, wg, wu, wd, gf, tm, tf, final_norm):
    n, d = x.shape
    dff = wg.shape[1]
    row = lambda i, j: (i, 0)
    const = lambda i, j: (0, 0)
    return pl.pallas_call(
        functools.partial(_ffn_body, final_norm=final_norm),
        grid=(n // tm, dff // tf),
        in_specs=[pl.BlockSpec((tm, d), row)] + [pl.BlockSpec((tm, W_GROUP), row)] * 4
                 + [pl.BlockSpec(wo.shape, const), pl.BlockSpec((1, d), const),
                    pl.BlockSpec((d, tf), lambda i, j: (0, j)), pl.BlockSpec((d, tf), lambda i, j: (0, j)),
                    pl.BlockSpec((tf, d), lambda i, j: (j, 0)), pl.BlockSpec((1, d), const)],
        out_specs=pl.BlockSpec((tm, d), row),
        out_shape=jax.ShapeDtypeStruct((n, d), F32),
        scratch_shapes=[pltpu.VMEM((tm, d), F32), pltpu.VMEM((tm, d), BF16)],
        compiler_params=_params(("parallel", "arbitrary")),
        name="outproj_ffn",
    )(x, *ys, wo, gn, wg, wu, wd, gf)


def _pick(n, pref):
    t = min(pref, n)
    while n % t:
        t //= 2
    return t


def _layer_weights(l, norm_mix_g, w_in, w_out, lru_conv_w, lru_conv_b, lru_w_a, lru_b_a, lru_w_x, lru_b_x,
                   lru_lambda, sc_conv_w, fox_b_f, diff_lambda, diff_norm_g, norm_ffn_g, w_gate, w_up, w_down):
    d = w_in.shape[1]
    wi = w_in[l]
    n_fl = N_HEADS
    c_fl = 8 * W_GROUP
    wi = jnp.concatenate([wi[:, :c_fl], wi[:, c_fl + n_fl:], wi[:, c_fl:c_fl + n_fl],
                          jnp.zeros((d, LANES - n_fl), wi.dtype)], axis=1).astype(BF16)
    gb = lru_w_a.shape[-1]

    def block_diag(w):
        out = jnp.zeros((W_GROUP, W_GROUP), w.dtype)
        for h in range(w.shape[0]):
            out = out.at[h * gb:(h + 1) * gb, h * gb:(h + 1) * gb].set(w[h])
        return out

    wgates = jnp.concatenate([block_diag(lru_w_a[l]), block_diag(lru_w_x[l])], axis=1).astype(BF16)
    bgates = jnp.concatenate([lru_b_a[l], lru_b_x[l]])[None, :]
    bf = jnp.concatenate([fox_b_f[l], jnp.zeros((LANES - N_HEADS,), F32)])[None, :]
    return dict(
        gmix=norm_mix_g[l][None, :], wi=wi, wo=w_out[l].astype(BF16),
        cw=lru_conv_w[l], cb=lru_conv_b[l][None, :], wgates=wgates, bgates=bgates, lam=lru_lambda[l][None, :],
        scw=sc_conv_w[l], bf=bf, dlam=diff_lambda[l],
        g_pair=jnp.tile(diff_norm_g[l], 2)[None, :], g_full=jnp.tile(diff_norm_g[l], N_HEADS)[None, :],
        gffn=norm_ffn_g[l][None, :], wg=w_gate[l].astype(BF16), wu=w_up[l].astype(BF16),
        wd=w_down[l].astype(BF16))


def kernel(x_prompt, x_sample, cache_fox_k, cache_fox_v, cache_fox_logf, cache_diff_k, cache_diff_v, state_lru_h, state_lru_conv, state_sconv, page_table, norm_mix_g, w_in, w_out, lru_conv_w, lru_conv_b, lru_w_a, lru_b_a, lru_w_x, lru_b_x, lru_lambda, sc_conv_w, fox_b_f, diff_lambda, diff_norm_g, rel_bias, norm_ffn_g, w_gate, w_up, w_down, norm_final_g):
    depth = w_in.shape[0]
    gp, tp, d = x_prompt.shape
    gs, ts, _ = x_sample.shape
    n_pool, page = cache_fox_k.shape[1], cache_fox_k.shape[2]
    n_pages = page_table.shape[1]
    dff = w_gate.shape[2]
    assert ts % SUBLANES == 0 and ts & (ts - 1) == 0 and ts <= page and page == LANES and MAX_DISTANCE <= page

    tq = _pick(tp, 256)
    tk = tq
    tc = _pick(tp, 256)
    tm_p = _pick(gp * tp, 512)
    tm_s = _pick(gs * ts, 512)
    tf = dff // 2 if (dff // 2) % LANES == 0 else dff
    pp = _pick(n_pages, 4)

    strip = _bias_tile(rel_bias, tq, tq + tk, tk)
    t15 = _bias_tile(rel_bias, ts, page, page)
    tnew = _bias_tile(rel_bias, ts, page, 0)
    gf = norm_final_g[None, :]

    cfk = cache_fox_k.reshape(depth, n_pool, page, W_GROUP)
    cfv = cache_fox_v.reshape(depth, n_pool, page, W_GROUP)
    cdk = cache_diff_k.reshape(depth, n_pool, page, W_GROUP)
    cdv = cache_diff_v.reshape(depth, n_pool, page, W_GROUP)
    clf_t = jnp.swapaxes(cache_fox_logf, 2, 3)

    xp = x_prompt.reshape(gp * tp, d)
    xs = x_sample.reshape(gs * ts, d)
    p_states, s_states = [], []
    for l in range(depth):
        w = _layer_weights(l, norm_mix_g, w_in, w_out, lru_conv_w, lru_conv_b, lru_w_a, lru_b_a, lru_w_x,
                           lru_b_x, lru_lambda, sc_conv_w, fox_b_f, diff_lambda, diff_norm_g, norm_ffn_g,
                           w_gate, w_up, w_down)
        li = _lambda_init(l)
        final = l == depth - 1

        (zl, zs, fl, fqb, fkb, fvb, dqb, dkb, dvb, fk, fv, dk, dv) = _inproj(xp, w["gmix"], w["wi"], tm_p)
        ylru, ysc, logf, cumf, hlast, lbufn, sbufn = _mix(
            zl, zs, fl, jnp.zeros((gp, 1, W_GROUP), F32), jnp.zeros((gp, LRU_CONV - 1, W_GROUP), F32),
            jnp.zeros((gp, SC_CONV - 1, W_GROUP), F32), w["cw"], w["cb"], w["wgates"], w["bgates"], w["lam"],
            w["scw"], w["bf"], gp, tp, tc)
        fk_rows = jnp.swapaxes(cumf[:, :N_HEADS].reshape(gp, tp, N_HEADS), 1, 2)
        yfox = _fox_prompt(fqb, fkb, fvb, fk_rows, gp, tp, tq, tk)
        ydiff = _diff_prompt(dqb, dkb, dvb, strip, w["dlam"], w["g_pair"], gp, tp, tq, tk, li)
        xp = _outproj_ffn(xp, (ylru, ysc, yfox, ydiff), w["wo"], w["gffn"], w["wg"], w["wu"], w["wd"], gf,
                          tm_p, tf, final)
        p_states.append((fk.reshape(gp, tp, N_HEADS, DH), fv.reshape(gp, tp, N_HEADS, DH),
                         logf[:, :N_HEADS].reshape(gp, tp, N_HEADS),
                         dk.reshape(gp, tp, N_HEADS, DH), dv.reshape(gp, tp, N_HEADS, DH),
                         hlast.reshape(gp, W_GROUP), lbufn, sbufn))

        (zl, zs, fl, fqb, fkb, fvb, dqb, dkb, dvb, fk, fv, dk, dv) = _inproj(xs, w["gmix"], w["wi"], tm_s)
        ylru, ysc, logf, cumf, hlast, lbufn, sbufn = _mix(
            zl, zs, fl, state_lru_h[l][:, None, :], state_lru_conv[l], state_sconv[l],
            w["cw"], w["cb"], w["wgates"], w["bgates"], w["lam"], w["scw"], w["bf"], gs, ts, ts)
        ncum = -jnp.swapaxes(cumf[:, :N_HEADS].reshape(gs, ts, N_HEADS), 1, 2)
        ncum = jnp.pad(ncum, ((0, 0), (0, 0), (0, page - ts)))
        r3 = lambda a: a.reshape(gs, ts, W_GROUP)
        yfox, ydiff = _sample_attn(page_table, r3(fqb), r3(dqb), r3(fk), r3(fv), r3(dk), r3(dv), ncum, t15, tnew,
                                   w["dlam"], w["g_full"], cfk, cfv, cdk, cdv, clf_t, l, gs, ts, pp, li)
        xs = _outproj_ffn(xs, (ylru, ysc, yfox.reshape(gs * ts, W_GROUP), ydiff.reshape(gs * ts, W_GROUP)),
                          w["wo"], w["gffn"], w["wg"], w["wu"], w["wd"], gf, tm_s, tf, final)
        s_states.append((fk.reshape(gs, ts, N_HEADS, DH), fv.reshape(gs, ts, N_HEADS, DH),
                         logf[:, :N_HEADS].reshape(gs, ts, N_HEADS),
                         dk.reshape(gs, ts, N_HEADS, DH), dv.reshape(gs, ts, N_HEADS, DH),
                         hlast.reshape(gs, W_GROUP), lbufn, sbufn))

    y_prompt = xp.reshape(gp, tp, d)
    y_sample = xs.reshape(gs, ts, d)
    p_out = [jnp.stack(s, axis=0) for s in zip(*p_states)]
    s_out = [jnp.stack(s, axis=0) for s in zip(*s_states)]
    return (y_prompt, y_sample, *p_out, *s_out)
```

```python
import functools
import math

import jax
import jax.numpy as jnp
from jax import lax
from jax.experimental import pallas as pl
from jax.experimental.pallas import tpu as pltpu

F32 = jnp.float32
BF16 = jnp.bfloat16

N_HEADS = 4
W_GROUP = 256
DH = W_GROUP // N_HEADS
DH_DIFF = DH // 2
LRU_CONV = 4
SC_CONV = 3
LRU_C = 8.0
N_BUCKETS = 32
MAX_DISTANCE = 128
EPS = 1e-6
FOX_SCALE = DH ** -0.5
DIFF_SCALE = DH_DIFF ** -0.5
NEG = -1e30
DH_SHIFT = DH.bit_length() - 1
DHD_SHIFT = DH_DIFF.bit_length() - 1
assert DH == 1 << DH_SHIFT and DH_DIFF == 1 << DHD_SHIFT

LANES = 128
SUBLANES = 8
VMEM_LIMIT = 48 * 1024 * 1024

C_LRU = (0, 2 * W_GROUP)
C_SC = (C_LRU[1], C_LRU[1] + 3 * W_GROUP)
C_FQ = (C_SC[1], C_SC[1] + W_GROUP)
C_FK = (C_FQ[1], C_FQ[1] + W_GROUP)
C_FV = (C_FK[1], C_FK[1] + W_GROUP)
C_DQ = (C_FV[1], C_FV[1] + W_GROUP)
C_DK = (C_DQ[1], C_DQ[1] + W_GROUP)
C_DV = (C_DK[1], C_DK[1] + W_GROUP)
C_FL = (C_DV[1], C_DV[1] + LANES)
N_IN_PAD = C_FL[1]


def _lambda_init(layer_idx):
    return 0.8 - 0.6 * math.exp(-0.3 * layer_idx)


def _params(sem):
    return pltpu.CompilerParams(dimension_semantics=sem, vmem_limit_bytes=VMEM_LIMIT)


def _inproj_body(x_ref, g_ref, w_ref, wt_ref, zl_ref, zs_ref, fl_ref, fk_ref, fv_ref, dk_ref, dv_ref, *rest,
                 feature_major):
    xn_ref = rest[-1]
    x = x_ref[...]
    xn = x * lax.rsqrt(jnp.mean(x * x, axis=-1, keepdims=True) + EPS) * g_ref[...]
    xn_ref[...] = xn.astype(BF16)

    def seg(c):
        return jnp.dot(xn_ref[...], w_ref[:, c[0]:c[1]], preferred_element_type=F32)

    def seg_t(i):
        return lax.dot_general(wt_ref[i * W_GROUP:(i + 1) * W_GROUP, :], xn_ref[...], (((1,), (1,)), ((), ())),
                               preferred_element_type=F32)

    zl_ref[...] = seg(C_LRU)
    zs_ref[...] = seg(C_SC)
    fl_ref[...] = seg(C_FL)
    fv_ref[...] = seg(C_FV)
    dv_ref[...] = seg(C_DV)
    if feature_major:
        fkb_ref, dkb_ref, fqt_ref, fvt_ref, dqt_ref, dvt_ref = rest[:-1]
        z = seg(C_FK)
        fk_ref[...] = z
        fkb_ref[...] = z.astype(BF16)
        z = seg(C_DK)
        dk_ref[...] = z
        dkb_ref[...] = z.astype(BF16)
        fqt_ref[...] = (seg_t(0) * FOX_SCALE).astype(BF16)
        fvt_ref[...] = seg_t(1).astype(BF16)
        dqt_ref[...] = (seg_t(2) * DIFF_SCALE).astype(BF16)
        dvt_ref[...] = seg_t(3).astype(BF16)
    else:
        fqb_ref, dqb_ref = rest[:-1]
        fk_ref[...] = seg(C_FK)
        dk_ref[...] = seg(C_DK)
        fqb_ref[...] = (seg(C_FQ) * FOX_SCALE).astype(BF16)
        dqb_ref[...] = (seg(C_DQ) * DIFF_SCALE).astype(BF16)


def _inproj(x, g, w, wt, tm, feature_major):
    n, d = x.shape
    row = lambda i: (i, 0)
    col = lambda i: (0, i)
    const = lambda i: (0, 0)
    outs = [((n, 2 * W_GROUP), F32, (tm, 2 * W_GROUP), row), ((n, 3 * W_GROUP), F32, (tm, 3 * W_GROUP), row),
            ((n, LANES), F32, (tm, LANES), row)] + [((n, W_GROUP), F32, (tm, W_GROUP), row)] * 4
    if feature_major:
        outs += [((n, W_GROUP), BF16, (tm, W_GROUP), row)] * 2 + [((W_GROUP, n), BF16, (W_GROUP, tm), col)] * 4
    else:
        outs += [((n, W_GROUP), BF16, (tm, W_GROUP), row)] * 2
    return pl.pallas_call(
        functools.partial(_inproj_body, feature_major=feature_major),
        grid=(n // tm,),
        in_specs=[pl.BlockSpec((tm, d), row), pl.BlockSpec((1, d), const), pl.BlockSpec((d, N_IN_PAD), const),
                  pl.BlockSpec(wt.shape, const)],
        out_specs=[pl.BlockSpec(blk, im) for _, _, blk, im in outs],
        out_shape=[jax.ShapeDtypeStruct(shp, dt) for shp, dt, _, _ in outs],
        scratch_shapes=[pltpu.VMEM((tm, d), BF16)],
        compiler_params=_params(("parallel",)),
        name="inproj",
    )(x, g, w, wt)


def _shift_rows(x, s, fill):
    row = lax.broadcasted_iota(jnp.int32, x.shape, 0)
    return jnp.where(row >= s, pltpu.roll(x, s, axis=0), fill)


def _mix_body(zl_ref, zs_ref, fl_ref, h0_ref, lbuf_ref, sbuf_ref, cw_ref, cb_ref, wg_ref, bg_ref, lam_ref,
              scw_ref, bf_ref, sel_ref,
              ylru_ref, ysc_ref, logf_ref, cumf_ref, aug_ref, hlast_ref, lbufn_ref, sbufn_ref,
              xpl_ref, xps_ref, hc_ref, fc_ref, *, tc):
    i = pl.program_id(1)
    last = pl.num_programs(1) - 1
    pad = SUBLANES

    @pl.when(i == 0)
    def _():
        xpl_ref[0:pad, :] = jnp.zeros((pad, W_GROUP), F32)
        xps_ref[0:pad, :] = jnp.zeros((pad, W_GROUP), F32)
        xpl_ref[pad - (LRU_CONV - 1):pad, :] = lbuf_ref[0]
        xps_ref[pad - (SC_CONV - 1):pad, :] = sbuf_ref[0]
        hc_ref[...] = h0_ref[0]
        fc_ref[...] = jnp.zeros_like(fc_ref)

    xpl_ref[pad:pad + tc, :] = zl_ref[:, 0:W_GROUP]
    o = pad - (LRU_CONV - 1)
    xc = xpl_ref[o:o + tc, :] * cw_ref[0:1, :]
    for k in range(1, LRU_CONV):
        xc = xc + xpl_ref[o + k:o + k + tc, :] * cw_ref[k:k + 1, :]
    xc = xc + cb_ref[...]
    gz = jnp.dot(xc.astype(BF16), wg_ref[...], preferred_element_type=F32) + bg_ref[...]
    r = jax.nn.sigmoid(gz[:, 0:W_GROUP])
    ig = jax.nn.sigmoid(gz[:, W_GROUP:2 * W_GROUP])
    nl = -lam_ref[...]
    sp = jnp.maximum(nl, 0.0) + jnp.log1p(jnp.exp(-jnp.abs(nl)))
    log_a = -LRU_C * r * sp
    a = jnp.exp(log_a)
    u = jnp.sqrt(1.0 - a * a) * ig * xc
    s = 1
    while s < tc:
        u = a * _shift_rows(u, s, 0.0) + u
        a = a * _shift_rows(a, s, 1.0)
        s *= 2
    h = a * hc_ref[...] + u
    hc_ref[...] = h[tc - 1:tc, :]
    ylru_ref[...] = (h * jax.nn.gelu(zl_ref[:, W_GROUP:2 * W_GROUP])).astype(BF16)

    xps_ref[pad:pad + tc, :] = zs_ref[:, W_GROUP:2 * W_GROUP] * zs_ref[:, 2 * W_GROUP:3 * W_GROUP]
    o = pad - (SC_CONV - 1)
    conv = xps_ref[o:o + tc, :] * scw_ref[0:1, :]
    for k in range(1, SC_CONV):
        conv = conv + xps_ref[o + k:o + k + tc, :] * scw_ref[k:k + 1, :]
    ysc_ref[...] = (zs_ref[:, 0:W_GROUP] * conv).astype(BF16)

    xf = fl_ref[...] + bf_ref[...]
    logf = jnp.minimum(xf, 0.0) - jnp.log1p(jnp.exp(-jnp.abs(xf)))
    logf_ref[...] = logf
    c = logf
    s = 1
    while s < tc:
        c = c + _shift_rows(c, s, 0.0)
        s *= 2
    c = c + fc_ref[...]
    cumf_ref[...] = c
    fc_ref[...] = c[tc - 1:tc, :]
    hi = c.astype(BF16)
    r1 = c - hi.astype(F32)
    mid = r1.astype(BF16)
    lo = (r1 - mid.astype(F32)).astype(BF16)
    aug_ref[...] = jnp.dot(jnp.concatenate([hi, mid, lo], axis=1), sel_ref[...],
                           preferred_element_type=F32).astype(BF16)

    @pl.when(i == last)
    def _():
        hlast_ref[0] = hc_ref[...]
        lbufn_ref[0] = xpl_ref[pad + tc - (LRU_CONV - 1):pad + tc, :]
        sbufn_ref[0] = xps_ref[pad + tc - (SC_CONV - 1):pad + tc, :]

    xpl_ref[0:pad, :] = xpl_ref[tc:tc + pad, :]
    xps_ref[0:pad, :] = xps_ref[tc:tc + pad, :]


def _bias_lane_selector():
    rows = lax.broadcasted_iota(jnp.int32, (3 * LANES, W_GROUP), 0)
    cols = lax.broadcasted_iota(jnp.int32, (3 * LANES, W_GROUP), 1)
    part, h = rows // LANES, rows % LANES
    target = (h // 2) * LANES + 3 * (h % 2) + part
    return ((h < N_HEADS) & (cols == target)).astype(BF16)


def _mix(zl, zs, fl, h0, lbuf, sbuf, cw, cb, wg, bg, lam, scw, bf, g, t, tc):
    sel = _bias_lane_selector()
    n = g * t
    nt = t // tc
    row = lambda b, i: (b * nt + i, 0)
    st = lambda b, i: (b, 0, 0)
    const = lambda b, i: (0, 0)
    full = lambda a: pl.BlockSpec(a.shape, const)
    return pl.pallas_call(
        functools.partial(_mix_body, tc=tc),
        grid=(g, nt),
        in_specs=[pl.BlockSpec((tc, 2 * W_GROUP), row), pl.BlockSpec((tc, 3 * W_GROUP), row),
                  pl.BlockSpec((tc, LANES), row),
                  pl.BlockSpec((1, 1, W_GROUP), st), pl.BlockSpec((1, LRU_CONV - 1, W_GROUP), st),
                  pl.BlockSpec((1, SC_CONV - 1, W_GROUP), st),
                  full(cw), full(cb), full(wg), full(bg), full(lam), full(scw), full(bf), full(sel)],
        out_specs=[pl.BlockSpec((tc, W_GROUP), row), pl.BlockSpec((tc, W_GROUP), row),
                   pl.BlockSpec((tc, LANES), row), pl.BlockSpec((tc, LANES), row),
                   pl.BlockSpec((tc, W_GROUP), row),
                   pl.BlockSpec((1, 1, W_GROUP), st), pl.BlockSpec((1, LRU_CONV - 1, W_GROUP), st),
                   pl.BlockSpec((1, SC_CONV - 1, W_GROUP), st)],
        out_shape=[jax.ShapeDtypeStruct((n, W_GROUP), BF16), jax.ShapeDtypeStruct((n, W_GROUP), BF16),
                   jax.ShapeDtypeStruct((n, LANES), F32), jax.ShapeDtypeStruct((n, LANES), F32),
                   jax.ShapeDtypeStruct((n, W_GROUP), BF16),
                   jax.ShapeDtypeStruct((g, 1, W_GROUP), F32),
                   jax.ShapeDtypeStruct((g, LRU_CONV - 1, W_GROUP), F32),
                   jax.ShapeDtypeStruct((g, SC_CONV - 1, W_GROUP), F32)],
        scratch_shapes=[pltpu.VMEM((tc + SUBLANES, W_GROUP), F32), pltpu.VMEM((tc + SUBLANES, W_GROUP), F32),
                        pltpu.VMEM((1, W_GROUP), F32), pltpu.VMEM((1, LANES), F32)],
        compiler_params=_params(("parallel", "arbitrary")),
        name="mix_scan",
    )(zl, zs, fl, h0, lbuf, sbuf, cw, cb, wg, bg, lam, scw, bf, sel)


def _bias_body(rb_ref, o_ref, *, off, keys_on_rows):
    _, rows, cols = o_ref.shape
    qdim, kdim = (1, 0) if keys_on_rows else (0, 1)
    rel = (off + lax.broadcasted_iota(jnp.int32, (rows, cols), qdim)
           - lax.broadcasted_iota(jnp.int32, (rows, cols), kdim))
    nn = jnp.maximum(rel, 0)
    max_exact = N_BUCKETS // 2
    nf = jnp.maximum(nn, 1).astype(F32)
    large = max_exact + (jnp.log(nf / max_exact) / math.log(MAX_DISTANCE / max_exact)
                         * (N_BUCKETS - max_exact)).astype(jnp.int32)
    large = jnp.minimum(large, N_BUCKETS - 1)
    bucket = jnp.where(nn < max_exact, nn, large)
    for h in range(N_HEADS):
        val = jnp.zeros((rows, cols), F32)
        for b in range(N_BUCKETS):
            val = jnp.where(bucket == b, rb_ref[b, h], val)
        o_ref[h] = jnp.where(rel >= 0, val - rb_ref[N_BUCKETS - 1, h], NEG)


def _bias_tile(rel_bias, rows, cols, off, keys_on_rows=False):
    return pl.pallas_call(
        functools.partial(_bias_body, off=off, keys_on_rows=keys_on_rows),
        in_specs=[pl.BlockSpec(memory_space=pltpu.SMEM)],
        out_specs=pl.BlockSpec(memory_space=pltpu.VMEM),
        out_shape=jax.ShapeDtypeStruct((N_HEADS, rows, cols), F32),
        name="t5_bias_tile",
    )(rel_bias)


def _flash_update(q, k, v, bias, m_ref, l_ref, acc_ref):
    s = lax.dot_general(q, k, (((1,), (1,)), ((), ())), preferred_element_type=F32)
    if bias is not None:
        s = s + bias
    m_prev = m_ref[...]
    m_new = jnp.maximum(m_prev, jnp.max(s, axis=-1, keepdims=True))
    alpha = jnp.exp(m_prev - m_new)
    p = jnp.exp(s - m_new)
    l_ref[...] = alpha * l_ref[...] + jnp.sum(p, axis=-1, keepdims=True)
    acc_ref[...] = alpha * acc_ref[...] + jnp.dot(p.astype(BF16), v, preferred_element_type=F32)
    m_ref[...] = m_new


def _pair_lane():
    return lax.broadcasted_iota(jnp.int32, (1, LANES), 1)


def _diff_lambda(dl_ref, lambda_init):
    dl = dl_ref[...]
    s01 = jnp.sum(dl[0:1, :] * dl[1:2, :], axis=-1, keepdims=True)
    s23 = jnp.sum(dl[2:3, :] * dl[3:4, :], axis=-1, keepdims=True)
    return jnp.exp(s01) - jnp.exp(s23) + lambda_init


def _fox_body(q_ref, k_ref, v_ref, fk_ref, o_ref, m_ref, l_ref, acc_ref, *, tq, tk):
    qi = pl.program_id(1)
    lane = _pair_lane()
    n_diag = tq // tk
    n_full = qi * n_diag
    qs = []
    for h in range(N_HEADS):
        p, e = divmod(h, 2)
        qp = q_ref[:, p * LANES:(p + 1) * LANES]
        qs.append(jnp.where((lane >> DH_SHIFT) == e, qp, jnp.zeros_like(qp)))
        m_ref[h] = jnp.full((tq, 1), NEG, F32)
        l_ref[h] = jnp.zeros((tq, 1), F32)
        acc_ref[h] = jnp.zeros((tq, LANES), F32)

    def block(k0, mask):
        for h in range(N_HEADS):
            p = h // 2
            k = k_ref[pl.ds(k0, tk), p * LANES:(p + 1) * LANES]
            v = v_ref[pl.ds(k0, tk), p * LANES:(p + 1) * LANES]
            bias = -fk_ref[h:h + 1, pl.ds(k0, tk)]
            if mask is not None:
                bias = jnp.where(mask, bias, NEG)
            _flash_update(qs[h], k, v, bias, m_ref.at[h], l_ref.at[h], acc_ref.at[h])

    def far(kj, carry):
        block(pl.multiple_of(kj * tk, tk), None)
        return carry

    lax.fori_loop(0, n_full, far, 0)
    row = lax.broadcasted_iota(jnp.int32, (tq, tk), 0)
    col = lax.broadcasted_iota(jnp.int32, (tq, tk), 1)
    for d in range(n_diag):
        block(pl.multiple_of(qi * tq + d * tk, tk), col + d * tk <= row)

    for p in range(N_HEADS // 2):
        o0 = acc_ref[2 * p] / l_ref[2 * p]
        o1 = acc_ref[2 * p + 1] / l_ref[2 * p + 1]
        o_ref[:, p * LANES:(p + 1) * LANES] = jnp.where(lane < DH, o0, o1).astype(BF16)


def _fox_prompt(q, k, v, fk_rows, g, t, tq, tk):
    nq = t // tq
    return pl.pallas_call(
        functools.partial(_fox_body, tq=tq, tk=tk),
        grid=(g, nq),
        in_specs=[pl.BlockSpec((tq, W_GROUP), lambda b, i: (b * nq + i, 0)),
                  pl.BlockSpec((t, W_GROUP), lambda b, i: (b, 0)),
                  pl.BlockSpec((t, W_GROUP), lambda b, i: (b, 0)),
                  pl.BlockSpec((None, N_HEADS, t), lambda b, i: (b, 0, 0))],
        out_specs=pl.BlockSpec((tq, W_GROUP), lambda b, i: (b * nq + i, 0)),
        out_shape=jax.ShapeDtypeStruct((g * t, W_GROUP), BF16),
        scratch_shapes=[pltpu.VMEM((N_HEADS, tq, 1), F32), pltpu.VMEM((N_HEADS, tq, 1), F32),
                        pltpu.VMEM((N_HEADS, tq, LANES), F32)],
        compiler_params=_params(("parallel", "arbitrary")),
        name="fox_prompt",
    )(q, k, v, fk_rows)


def _head_rmsnorm_pair(od, lane, g_row, lambda_init):
    sq = od * od
    lo = lane < DH
    ms0 = jnp.sum(jnp.where(lo, sq, 0.0), axis=-1, keepdims=True) * (1.0 / DH)
    ms1 = jnp.sum(jnp.where(lo, 0.0, sq), axis=-1, keepdims=True) * (1.0 / DH)
    inv = jnp.where(lo, lax.rsqrt(ms0 + EPS), lax.rsqrt(ms1 + EPS))
    return od * inv * g_row * (1.0 - lambda_init)


def _diffattn_body(q_ref, k_ref, v_ref, strip_ref, dl_ref, g_ref, o_ref, m_ref, l_ref, acc_ref,
                   *, tq, tk, lambda_init):
    qi = pl.program_id(1)
    lane = _pair_lane()
    n_diag = tq // tk
    n_full = qi * n_diag
    qs = []
    for h in range(N_HEADS):
        p, e = divmod(h, 2)
        qp = q_ref[:, p * LANES:(p + 1) * LANES]
        zero = jnp.zeros_like(qp)
        q1 = jnp.where((lane >> DHD_SHIFT) == 2 * e, qp, zero)
        q2 = jnp.where((lane >> DHD_SHIFT) == 2 * e + 1, qp, zero)
        qs.append(jnp.concatenate([q1, q2], axis=0))
        m_ref[h] = jnp.full((2 * tq, 1), NEG, F32)
        l_ref[h] = jnp.zeros((2 * tq, 1), F32)
        acc_ref[h] = jnp.zeros((2 * tq, LANES), F32)

    def block(k0, c0):
        for h in range(N_HEADS):
            p = h // 2
            k = k_ref[pl.ds(k0, tk), p * LANES:(p + 1) * LANES]
            v = v_ref[pl.ds(k0, tk), p * LANES:(p + 1) * LANES]
            bias = None
            if c0 is not None:
                b1 = strip_ref[h, :, c0:c0 + tk]
                bias = jnp.concatenate([b1, b1], axis=0)
            _flash_update(qs[h], k, v, bias, m_ref.at[h], l_ref.at[h], acc_ref.at[h])

    def far(kj, carry):
        block(pl.multiple_of(kj * tk, tk), None)
        return carry

    lax.fori_loop(0, jnp.maximum(n_full - 1, 0), far, 0)

    @pl.when(n_full > 0)
    def _():
        block(pl.multiple_of((n_full - 1) * tk, tk), 0)

    for d in range(n_diag):
        block(pl.multiple_of(qi * tq + d * tk, tk), (d + 1) * tk)

    lam = _diff_lambda(dl_ref, lambda_init)
    for p in range(N_HEADS // 2):
        ods = []
        for h in (2 * p, 2 * p + 1):
            o = acc_ref[h] / l_ref[h]
            ods.append(o[0:tq] - lam * o[tq:2 * tq])
        od = jnp.where(lane < DH, ods[0], ods[1])
        y = _head_rmsnorm_pair(od, lane, g_ref[...], lambda_init)
        o_ref[:, p * LANES:(p + 1) * LANES] = y.astype(BF16)


def _diff_prompt(q, k, v, strip, dlam, g_pair, g, t, tq, tk, lambda_init):
    nq = t // tq
    const2 = lambda b, i: (0, 0)
    return pl.pallas_call(
        functools.partial(_diffattn_body, tq=tq, tk=tk, lambda_init=lambda_init),
        grid=(g, nq),
        in_specs=[pl.BlockSpec((tq, W_GROUP), lambda b, i: (b * nq + i, 0)),
                  pl.BlockSpec((t, W_GROUP), lambda b, i: (b, 0)),
                  pl.BlockSpec((t, W_GROUP), lambda b, i: (b, 0)),
                  pl.BlockSpec(strip.shape, lambda b, i: (0, 0, 0)),
                  pl.BlockSpec(dlam.shape, const2), pl.BlockSpec(g_pair.shape, const2)],
        out_specs=pl.BlockSpec((tq, W_GROUP), lambda b, i: (b * nq + i, 0)),
        out_shape=jax.ShapeDtypeStruct((g * t, W_GROUP), BF16),
        scratch_shapes=[pltpu.VMEM((N_HEADS, 2 * tq, 1), F32), pltpu.VMEM((N_HEADS, 2 * tq, 1), F32),
                        pltpu.VMEM((N_HEADS, 2 * tq, LANES), F32)],
        compiler_params=_params(("parallel", "arbitrary")),
        name="diff_prompt",
    )(q, k, v, strip, dlam, g_pair)


def _attn_t_body(qt_ref, k_ref, vt_ref, *rest, tq, tk, fox, lambda_init):
    if fox:
        aug_ref, o_ref, m_ref, l_ref, acc_ref = rest
    else:
        strip_ref, dl_ref, g_ref, o_ref, m_ref, l_ref, acc_ref = rest
    qi = pl.program_id(1)
    n_diag = tq // tk
    n_full = qi * n_diag
    rq = tq if fox else 2 * tq
    frow = lax.broadcasted_iota(jnp.int32, (LANES, tq), 0)
    qts = []
    for h in range(N_HEADS):
        p, e = divmod(h, 2)
        qp = qt_ref[p * LANES:(p + 1) * LANES, :]
        zero = jnp.zeros_like(qp)
        if fox:
            qh = jnp.where((frow >> DH_SHIFT) == e, qp, zero)
            neg = jnp.where((frow >= 3 * e) & (frow < 3 * e + 3), -1.0, 0.0).astype(BF16)
            qts.append(jnp.concatenate([qh, neg], axis=0))
        else:
            q1 = jnp.where((frow >> DHD_SHIFT) == 2 * e, qp, zero)
            q2 = jnp.where((frow >> DHD_SHIFT) == 2 * e + 1, qp, zero)
            qts.append(jnp.concatenate([q1, q2], axis=1))
        m_ref[h] = jnp.full((1, rq), NEG, F32)
        l_ref[h] = jnp.zeros((1, rq), F32)
        acc_ref[h] = jnp.zeros((LANES, rq), F32)

    def block(k0, mask, c0):
        sts = []
        for h in range(N_HEADS):
            p = h // 2
            kt = k_ref[pl.ds(k0, tk), p * LANES:(p + 1) * LANES]
            if fox:
                kt = jnp.concatenate([kt, aug_ref[pl.ds(k0, tk), p * LANES:(p + 1) * LANES]], axis=1)
            st = jnp.dot(kt, qts[h], preferred_element_type=F32)
            if mask is not None:
                st = jnp.where(mask, st, NEG)
            if c0 is not None:
                b1 = strip_ref[h, c0:c0 + tk, :]
                st = st + jnp.concatenate([b1, b1], axis=1)
            sts.append(st)
        pts, alphas = [], []
        for h in range(N_HEADS):
            m_prev = m_ref[h]
            m_new = jnp.maximum(m_prev, jnp.max(sts[h], axis=0, keepdims=True))
            alpha = jnp.exp(m_prev - m_new)
            pt = jnp.exp(sts[h] - m_new)
            l_ref[h] = alpha * l_ref[h] + jnp.sum(pt, axis=0, keepdims=True)
            m_ref[h] = m_new
            pts.append(pt.astype(BF16))
            alphas.append(alpha)
        for h in range(N_HEADS):
            p = h // 2
            vt = vt_ref[p * LANES:(p + 1) * LANES, pl.ds(k0, tk)]
            acc_ref[h] = alphas[h] * acc_ref[h] + jnp.dot(vt, pts[h], preferred_element_type=F32)

    def far(kj, carry):
        block(pl.multiple_of(kj * tk, tk), None, None)
        return carry

    key = lax.broadcasted_iota(jnp.int32, (tk, tq), 0)
    qry = lax.broadcasted_iota(jnp.int32, (tk, tq), 1)
    if fox:
        lax.fori_loop(0, n_full, far, 0)
        for d in range(n_diag):
            block(pl.multiple_of(qi * tq + d * tk, tk), key + d * tk <= qry, None)
    else:
        lax.fori_loop(0, jnp.maximum(n_full - 1, 0), far, 0)

        @pl.when(n_full > 0)
        def _():
            block(pl.multiple_of((n_full - 1) * tk, tk), None, 0)

        for d in range(n_diag):
            block(pl.multiple_of(qi * tq + d * tk, tk), None, (d + 1) * tk)

    lo_half = frow < DH
    for p in range(N_HEADS // 2):
        outs = []
        for h in (2 * p, 2 * p + 1):
            o = acc_ref[h] / l_ref[h]
            if not fox:
                o = o[:, 0:tq] - _diff_lambda(dl_ref, lambda_init) * o[:, tq:2 * tq]
            outs.append(o)
        ot = jnp.where(lo_half, outs[0], outs[1])
        if fox:
            o_ref[:, p * LANES:(p + 1) * LANES] = ot.T.astype(BF16)
        else:
            sq = ot * ot
            ms0 = jnp.sum(sq[0:DH], axis=0, keepdims=True) * (1.0 / DH)
            ms1 = jnp.sum(sq[DH:2 * DH], axis=0, keepdims=True) * (1.0 / DH)
            inv = jnp.where(lo_half, lax.rsqrt(ms0 + EPS), lax.rsqrt(ms1 + EPS))
            y = (ot * inv).T * g_ref[...] * (1.0 - lambda_init)
            o_ref[:, p * LANES:(p + 1) * LANES] = y.astype(BF16)


def _attn_prompt(qt, k, vt, extra, g, t, tq, tk, fox, lambda_init=0.0):
    nq = t // tq
    rq = tq if fox else 2 * tq
    blk_q = lambda b, i: (0, b * nq + i)
    tok = lambda b, i: (b, 0)
    in_specs = [pl.BlockSpec((W_GROUP, tq), blk_q), pl.BlockSpec((t, W_GROUP), tok),
                pl.BlockSpec((W_GROUP, t), lambda b, i: (0, b))]
    if fox:
        in_specs += [pl.BlockSpec((t, W_GROUP), tok)]
    else:
        strip, dlam, g_pair = extra
        in_specs += [pl.BlockSpec(strip.shape, lambda b, i: (0, 0, 0)),
                     pl.BlockSpec(dlam.shape, lambda b, i: (0, 0)), pl.BlockSpec(g_pair.shape, lambda b, i: (0, 0))]
    return pl.pallas_call(
        functools.partial(_attn_t_body, tq=tq, tk=tk, fox=fox, lambda_init=lambda_init),
        grid=(g, nq),
        in_specs=in_specs,
        out_specs=pl.BlockSpec((tq, W_GROUP), lambda b, i: (b * nq + i, 0)),
        out_shape=jax.ShapeDtypeStruct((g * t, W_GROUP), BF16),
        scratch_shapes=[pltpu.VMEM((N_HEADS, 1, rq), F32), pltpu.VMEM((N_HEADS, 1, rq), F32),
                        pltpu.VMEM((N_HEADS, LANES, rq), F32)],
        compiler_params=_params(("parallel", "arbitrary")),
        name="fox_prompt" if fox else "diff_prompt",
    )(qt, k, vt, *extra)


def _sample_attn_body(pt_ref, fq_ref, dq_ref, fkn_ref, fvn_ref, dkn_ref, dvn_ref, ncum_ref, t15_ref, tnew_ref,
                      dl_ref, g_ref, *rest, ts, pp, page, lambda_init):
    del pt_ref
    pages = rest[:5 * pp]
    yf_ref, yd_ref = rest[5 * pp:5 * pp + 2]
    rf = N_HEADS * ts
    rd = 2 * rf

    ts_shift = ts.bit_length() - 1
    lane = lax.broadcasted_iota(jnp.int32, (rf, W_GROUP), 1)
    rowh = lax.broadcasted_iota(jnp.int32, (rf, W_GROUP), 0) >> ts_shift
    qf = fq_ref[0].astype(F32)
    qf = jnp.concatenate([qf] * N_HEADS, axis=0)
    qf = jnp.where((lane >> DH_SHIFT) == rowh, qf, 0.0).astype(BF16)
    qd = dq_ref[0].astype(F32)
    qd = jnp.concatenate([qd] * N_HEADS, axis=0)
    qd = jnp.concatenate([jnp.where((lane >> DHD_SHIFT) == 2 * rowh, qd, 0.0),
                          jnp.where((lane >> DHD_SHIFT) == 2 * rowh + 1, qd, 0.0)], axis=0).astype(BF16)

    def per_head_rows(x):
        return jnp.concatenate([jnp.broadcast_to(x[h:h + 1, :], (ts, x.shape[1])) for h in range(N_HEADS)], axis=0)

    def pad_rows(x):
        return jnp.concatenate([x, jnp.zeros((page - ts, x.shape[1]), x.dtype)], axis=0)

    nt_dims = (((1,), (1,)), ((), ()))

    def softmax_pv(tiles, values):
        m = tiles[0]
        for s in tiles[1:]:
            m = jnp.maximum(m, s)
        m = jnp.max(m, axis=-1, keepdims=True)
        ps = [jnp.exp(s - m) for s in tiles]
        l = ps[0]
        for p in ps[1:]:
            l = l + p
        l = jnp.sum(l, axis=-1, keepdims=True)
        acc = jnp.dot(ps[0].astype(BF16), values[0], preferred_element_type=F32)
        for p, v in zip(ps[1:], values[1:]):
            acc = acc + lax.dot_general(p.astype(BF16), v, nt_dims, preferred_element_type=F32)
        return acc / l

    r = lax.broadcasted_iota(jnp.int32, (rf, page), 0) & (ts - 1)
    c = lax.broadcasted_iota(jnp.int32, (rf, page), 1)
    kn = pad_rows(fkn_ref[0]).astype(BF16)
    s_new = lax.dot_general(qf, kn, nt_dims, preferred_element_type=F32)
    f_tiles = [jnp.where(c <= r, s_new + per_head_rows(ncum_ref[0]), NEG)]
    f_vals = [pad_rows(fvn_ref[0]).astype(BF16)]
    tn = tnew_ref[...].reshape(rf, page)
    dkn = pad_rows(dkn_ref[0]).astype(BF16)
    d_tiles = [lax.dot_general(qd, dkn, nt_dims, preferred_element_type=F32) + jnp.concatenate([tn, tn], axis=0)]
    d_vals = [pad_rows(dvn_ref[0]).astype(BF16)]

    klane = lax.broadcasted_iota(jnp.int32, (N_HEADS, page), 1)
    carry = jnp.zeros((N_HEADS, 1), F32)
    f_bias = [None] * pp
    for p in reversed(range(pp)):
        logf = pages[5 * p + 4][0, 0]
        sfx = logf
        s = 1
        while s < page:
            sfx = sfx + jnp.where(klane < page - s, pltpu.roll(sfx, page - s, axis=1), 0.0)
            s *= 2
        f_bias[p] = per_head_rows(sfx - logf + carry)
        carry = carry + sfx[:, 0:1]
    for p in range(pp):
        fk_p, fv_p, dk_p, dv_p = pages[5 * p:5 * p + 4]
        f_tiles.append(jnp.dot(qf, fk_p[0, 0].astype(BF16), preferred_element_type=F32) + f_bias[p])
        f_vals.append(fv_p[0, 0].astype(BF16))
        s = jnp.dot(qd, dk_p[0, 0].astype(BF16), preferred_element_type=F32)
        if p == pp - 1:
            t15 = t15_ref[...].reshape(rf, page)
            s = s + jnp.concatenate([t15, t15], axis=0)
        d_tiles.append(s)
        d_vals.append(dv_p[0, 0].astype(BF16))

    of = softmax_pv(f_tiles, f_vals)
    od_all = softmax_pv(d_tiles, d_vals)
    lane_o = lax.broadcasted_iota(jnp.int32, (ts, W_GROUP), 1)
    lam = _diff_lambda(dl_ref, lambda_init)
    od_all = od_all[0:rf] - lam * od_all[rf:rd]
    yf = jnp.zeros((ts, W_GROUP), F32)
    od = jnp.zeros((ts, W_GROUP), F32)
    inv = jnp.zeros((ts, W_GROUP), F32)
    for h in range(N_HEADS):
        sel = (lane_o >> DH_SHIFT) == h
        yf = jnp.where(sel, of[h * ts:(h + 1) * ts], yf)
        oh = jnp.where(sel, od_all[h * ts:(h + 1) * ts], 0.0)
        ms = jnp.sum(oh * oh, axis=-1, keepdims=True) * (1.0 / DH)
        od = od + oh
        inv = jnp.where(sel, lax.rsqrt(ms + EPS), inv)
    yf_ref[0] = yf.astype(BF16)
    yd_ref[0] = (od * inv * g_ref[...] * (1.0 - lambda_init)).astype(BF16)


def _sample_attn(page_table, fq, dq, fkn, fvn, dkn, dvn, ncum, t15, tnew, dlam, g_full,
                 cfk, cfv, cdk, cdv, clf_t, layer, g, ts, lambda_init):
    pp = page_table.shape[1]
    page = cfk.shape[3]

    tok = lambda b, pt: (b, 0, 0)
    const3 = lambda b, pt: (0, 0, 0)
    const2 = lambda b, pt: (0, 0)

    def page_map(p):
        return lambda b, pt: (layer, pt[b, p], 0, 0)

    in_specs = [pl.BlockSpec((1, ts, W_GROUP), tok)] * 6
    in_specs += [pl.BlockSpec((1, N_HEADS, page), tok), pl.BlockSpec(t15.shape, const3),
                 pl.BlockSpec(tnew.shape, const3), pl.BlockSpec(dlam.shape, const2),
                 pl.BlockSpec(g_full.shape, const2)]
    args = [fq, dq, fkn, fvn, dkn, dvn, ncum, t15, tnew, dlam, g_full]
    for p in range(pp):
        in_specs += [pl.BlockSpec((1, 1, W_GROUP, page), page_map(p))] * 4
        in_specs += [pl.BlockSpec((1, 1, N_HEADS, page), page_map(p))]
        args += [cfk, cfv, cdk, cdv, clf_t]
    grid_spec = pltpu.PrefetchScalarGridSpec(
        num_scalar_prefetch=1,
        grid=(g,),
        in_specs=in_specs,
        out_specs=[pl.BlockSpec((1, ts, W_GROUP), tok), pl.BlockSpec((1, ts, W_GROUP), tok)],
    )
    return pl.pallas_call(
        functools.partial(_sample_attn_body, ts=ts, pp=pp, page=page, lambda_init=lambda_init),
        grid_spec=grid_spec,
        out_shape=[jax.ShapeDtypeStruct((g, ts, W_GROUP), BF16)] * 2,
        compiler_params=_params(("parallel",)),
        name="sample_attn",
    )(page_table, *args)


def _ffn_body(x_ref, y0_ref, y1_ref, y2_ref, y3_ref, wo_ref, gn_ref, wg_ref, wu_ref, wd_ref, gf_ref,
              o_ref, acc_ref, xn_ref, *, final_norm):
    j = pl.program_id(1)

    @pl.when(j == 0)
    def _():
        x1 = x_ref[...]
        for gi, y_ref in enumerate((y0_ref, y1_ref, y2_ref, y3_ref)):
            x1 = x1 + jnp.dot(y_ref[...], wo_ref[gi * W_GROUP:(gi + 1) * W_GROUP, :], preferred_element_type=F32)
        acc_ref[...] = x1
        xn = x1 * lax.rsqrt(jnp.mean(x1 * x1, axis=-1, keepdims=True) + EPS) * gn_ref[...]
        xn_ref[...] = xn.astype(BF16)

    xn = xn_ref[...]
    gate = jnp.dot(xn, wg_ref[...], preferred_element_type=F32)
    up = jnp.dot(xn, wu_ref[...], preferred_element_type=F32)
    hid = (jax.nn.silu(gate) * up).astype(BF16)
    acc_ref[...] += jnp.dot(hid, wd_ref[...], preferred_element_type=F32)

    @pl.when(j == pl.num_programs(1) - 1)
    def _():
        x2 = acc_ref[...]
        if final_norm:
            x2 = x2 * lax.rsqrt(jnp.mean(x2 * x2, axis=-1, keepdims=True) + EPS) * gf_ref[...]
        o_ref[...] = x2


def _outproj_ffn(x, ys, wo, gn, wg, wu, wd, gf, tm, tf, final_norm):
    n, d = x.shape
    dff = wg.shape[1]
    row = lambda i, j: (i, 0)
    const = lambda i, j: (0, 0)
    return pl.pallas_call(
        functools.partial(_ffn_body, final_norm=final_norm),
        grid=(n // tm, dff // tf),
        in_specs=[pl.BlockSpec((tm, d), row)] + [pl.BlockSpec((tm, W_GROUP), row)] * 4
                 + [pl.BlockSpec(wo.shape, const), pl.BlockSpec((1, d), const),
                    pl.BlockSpec((d, tf), lambda i, j: (0, j)), pl.BlockSpec((d, tf), lambda i, j: (0, j)),
                    pl.BlockSpec((tf, d), lambda i, j: (j, 0)), pl.BlockSpec((1, d), const)],
        out_specs=pl.BlockSpec((tm, d), row),
        out_shape=jax.ShapeDtypeStruct((n, d), F32),
        scratch_shapes=[pltpu.VMEM((tm, d), F32), pltpu.VMEM((tm, d), BF16)],
        compiler_params=_params(("parallel", "arbitrary")),
        name="outproj_ffn",
    )(x, *ys, wo, gn, wg, wu, wd, gf)


def _pick(n, pref):
    t = min(pref, n)
    while n % t:
        t //= 2
    return t


def _layer_weights(l, norm_mix_g, w_in, w_out, lru_conv_w, lru_conv_b, lru_w_a, lru_b_a, lru_w_x, lru_b_x,
                   lru_lambda, sc_conv_w, fox_b_f, diff_lambda, diff_norm_g, norm_ffn_g, w_gate, w_up, w_down):
    d = w_in.shape[1]
    wi = w_in[l]
    n_fl = N_HEADS
    c_fl = 8 * W_GROUP
    wi = jnp.concatenate([wi[:, :c_fl], wi[:, c_fl + n_fl:], wi[:, c_fl:c_fl + n_fl],
                          jnp.zeros((d, LANES - n_fl), wi.dtype)], axis=1).astype(BF16)
    wt = jnp.concatenate([wi[:, c[0]:c[1]] for c in (C_FQ, C_FV, C_DQ, C_DV)], axis=1).T
    gb = lru_w_a.shape[-1]

    def block_diag(w):
        out = jnp.zeros((W_GROUP, W_GROUP), w.dtype)
        for h in range(w.shape[0]):
            out = out.at[h * gb:(h + 1) * gb, h * gb:(h + 1) * gb].set(w[h])
        return out

    wgates = jnp.concatenate([block_diag(lru_w_a[l]), block_diag(lru_w_x[l])], axis=1).astype(BF16)
    bgates = jnp.concatenate([lru_b_a[l], lru_b_x[l]])[None, :]
    bf = jnp.concatenate([fox_b_f[l], jnp.zeros((LANES - N_HEADS,), F32)])[None, :]
    return dict(
        gmix=norm_mix_g[l][None, :], wi=wi, wt=wt, wo=w_out[l].astype(BF16),
        cw=lru_conv_w[l], cb=lru_conv_b[l][None, :], wgates=wgates, bgates=bgates, lam=lru_lambda[l][None, :],
        scw=sc_conv_w[l], bf=bf, dlam=diff_lambda[l],
        g_pair=jnp.tile(diff_norm_g[l], 2)[None, :], g_full=jnp.tile(diff_norm_g[l], N_HEADS)[None, :],
        gffn=norm_ffn_g[l][None, :], wg=w_gate[l].astype(BF16), wu=w_up[l].astype(BF16),
        wd=w_down[l].astype(BF16))


def kernel(x_prompt, x_sample, cache_fox_k, cache_fox_v, cache_fox_logf, cache_diff_k, cache_diff_v, state_lru_h, state_lru_conv, state_sconv, page_table, norm_mix_g, w_in, w_out, lru_conv_w, lru_conv_b, lru_w_a, lru_b_a, lru_w_x, lru_b_x, lru_lambda, sc_conv_w, fox_b_f, diff_lambda, diff_norm_g, rel_bias, norm_ffn_g, w_gate, w_up, w_down, norm_final_g):
    depth = w_in.shape[0]
    gp, tp, d = x_prompt.shape
    gs, ts, _ = x_sample.shape
    n_pool, page = cache_fox_k.shape[1], cache_fox_k.shape[2]
    n_pages = page_table.shape[1]
    dff = w_gate.shape[2]
    assert ts % SUBLANES == 0 and ts & (ts - 1) == 0 and ts <= page and page == LANES and MAX_DISTANCE <= page

    tq = _pick(tp, 256)
    tk = tq
    assert tk >= MAX_DISTANCE
    tc = _pick(tp, 256)
    tm_p = _pick(gp * tp, 512)
    tm_s = _pick(gs * ts, 512)
    tf = dff // 2 if (dff // 2) % LANES == 0 else dff

    strip = _bias_tile(rel_bias, tq + tk, tq, tk, keys_on_rows=True)
    t15 = _bias_tile(rel_bias, ts, page, page)
    tnew = _bias_tile(rel_bias, ts, page, 0)
    gf = norm_final_g[None, :]

    def feature_major(c):
        return jnp.transpose(c, (0, 1, 3, 4, 2)).reshape(depth, n_pool, W_GROUP, page)

    cfk, cfv, cdk, cdv = (feature_major(c) for c in (cache_fox_k, cache_fox_v, cache_diff_k, cache_diff_v))
    clf_t = jnp.swapaxes(cache_fox_logf, 2, 3)

    xp = x_prompt.reshape(gp * tp, d)
    xs = x_sample.reshape(gs * ts, d)
    p_states, s_states = [], []
    for l in range(depth):
        w = _layer_weights(l, norm_mix_g, w_in, w_out, lru_conv_w, lru_conv_b, lru_w_a, lru_b_a, lru_w_x,
                           lru_b_x, lru_lambda, sc_conv_w, fox_b_f, diff_lambda, diff_norm_g, norm_ffn_g,
                           w_gate, w_up, w_down)
        li = _lambda_init(l)
        final = l == depth - 1

        (zl, zs, fl, fk, fv, dk, dv, fkb, dkb, fqt, fvt, dqt, dvt) = _inproj(
            xp, w["gmix"], w["wi"], w["wt"], tm_p, True)
        ylru, ysc, logf, cumf, aug, hlast, lbufn, sbufn = _mix(
            zl, zs, fl, jnp.zeros((gp, 1, W_GROUP), F32), jnp.zeros((gp, LRU_CONV - 1, W_GROUP), F32),
            jnp.zeros((gp, SC_CONV - 1, W_GROUP), F32), w["cw"], w["cb"], w["wgates"], w["bgates"], w["lam"],
            w["scw"], w["bf"], gp, tp, tc)
        yfox = _attn_prompt(fqt, fkb, fvt, (aug,), gp, tp, tq, tk, True)
        ydiff = _attn_prompt(dqt, dkb, dvt, (strip, w["dlam"], w["g_pair"]), gp, tp, tq, tk, False, li)
        xp = _outproj_ffn(xp, (ylru, ysc, yfox, ydiff), w["wo"], w["gffn"], w["wg"], w["wu"], w["wd"], gf,
                          tm_p, tf, final)
        p_states.append((fk.reshape(gp, tp, N_HEADS, DH), fv.reshape(gp, tp, N_HEADS, DH),
                         logf[:, :N_HEADS].reshape(gp, tp, N_HEADS),
                         dk.reshape(gp, tp, N_HEADS, DH), dv.reshape(gp, tp, N_HEADS, DH),
                         hlast.reshape(gp, W_GROUP), lbufn, sbufn))

        (zl, zs, fl, fk, fv, dk, dv, fqb, dqb) = _inproj(xs, w["gmix"], w["wi"], w["wt"], tm_s, False)
        ylru, ysc, logf, cumf, _, hlast, lbufn, sbufn = _mix(
            zl, zs, fl, state_lru_h[l][:, None, :], state_lru_conv[l], state_sconv[l],
            w["cw"], w["cb"], w["wgates"], w["bgates"], w["lam"], w["scw"], w["bf"], gs, ts, ts)
        ncum = -jnp.swapaxes(cumf[:, :N_HEADS].reshape(gs, ts, N_HEADS), 1, 2)
        ncum = jnp.pad(ncum, ((0, 0), (0, 0), (0, page - ts)))
        r3 = lambda a: a.reshape(gs, ts, W_GROUP)
        yfox, ydiff = _sample_attn(page_table, r3(fqb), r3(dqb), r3(fk), r3(fv), r3(dk), r3(dv), ncum, t15, tnew,
                                   w["dlam"], w["g_full"], cfk, cfv, cdk, cdv, clf_t, l, gs, ts, li)
        xs = _outproj_ffn(xs, (ylru, ysc, yfox.reshape(gs * ts, W_GROUP), ydiff.reshape(gs * ts, W_GROUP)),
                          w["wo"], w["gffn"], w["wg"], w["wu"], w["wd"], gf, tm_s, tf, final)
        s_states.append((fk.reshape(gs, ts, N_HEADS, DH), fv.reshape(gs, ts, N_HEADS, DH),
                         logf[:, :N_HEADS].reshape(gs, ts, N_HEADS),
                         dk.reshape(gs, ts, N_HEADS, DH), dv.reshape(gs, ts, N_HEADS, DH),
                         hlast.reshape(gs, W_GROUP), lbufn, sbufn))

    y_prompt = xp.reshape(gp, tp, d)
    y_sample = xs.reshape(gs, ts, d)
    p_out = [jnp.stack(s, axis=0) for s in zip(*p_states)]
    s_out = [jnp.stack(s, axis=0) for s in zip(*s_states)]
    return (y_prompt, y_sample, *p_out, *s_out)
```
